```python
import math
import jax, jax.numpy as jnp
from jax import lax
import numpy as np

D_MODEL = 1024
BATCH = 8
SEQ = 4096
DEPTH = 1
DEC_BATCH = 16
DEC_SEQ = 64
PAST_LEN = 1024

CHUNK = 64
D_CONV = 1024
CONV_A_WIDTH = 3
N_HEADS = 8
HEAD_K = 128
HEAD_V = 128
QK_DIM = N_HEADS * HEAD_K
V_DIM = N_HEADS * HEAD_V
QKV_DIM = 2 * QK_DIM + V_DIM
CONV_QKV_WIDTH = 4
IN_DIM = 3 * D_CONV + QKV_DIM + V_DIM + 2 * N_HEADS + 2 * D_MODEL
N_GROUPS = 4
EXPERTS_PER_GROUP = 8
N_EXPERTS = N_GROUPS * EXPERTS_PER_GROUP
TOP_K = 2
D_EXPERT = 256
EPS = 1e-6

kernel_name = "hybrid_conv_gdn_hmoe_stream_step"


def rmsnorm(x, g):
    xf = x.astype(jnp.float32)
    r = lax.rsqrt(jnp.mean(xf * xf, axis=-1, keepdims=True) + EPS)
    return (xf * r * g.astype(jnp.float32)).astype(x.dtype)


def causal_conv(x, prev, w):
    width = w.shape[0]
    L = x.shape[1]
    xp = jnp.concatenate([prev.astype(x.dtype), x], axis=1)
    y = sum(xp[:, j:j + L] * w[j] for j in range(width))
    return y, xp[:, -(width - 1):]


def l2norm(x):
    return x * lax.rsqrt(jnp.sum(x * x, axis=-1, keepdims=True) + EPS)


def gated_delta_chunked(q, k, v, g, beta, S0, chunk):
    B, L, H, DK = q.shape
    DV = v.shape[-1]
    n = L // chunk

    def split(t):
        t = t.reshape((B, n, chunk) + t.shape[2:])
        return jnp.moveaxis(t, 3, 1)

    q, k, v, g, beta = (split(t) for t in (q, k, v, g, beta))
    G = jnp.cumsum(g, axis=-1)
    idx = jnp.arange(chunk)
    incl = idx[:, None] >= idx[None, :]
    strict = idx[:, None] > idx[None, :]
    decay = jnp.exp(jnp.where(incl, G[..., :, None] - G[..., None, :], -jnp.inf))
    kk = jnp.einsum('bhnid,bhnjd->bhnij', k, k)
    A = jnp.where(strict, beta[..., :, None] * kk * decay, 0.0)
    eye = jnp.eye(chunk, dtype=q.dtype)
    T = lax.linalg.triangular_solve(eye + A, jnp.broadcast_to(eye, A.shape),
                                    left_side=True, lower=True, unit_diagonal=True)
    u = jnp.einsum('bhnij,bhnjd->bhnid', T, beta[..., None] * v)
    w = jnp.einsum('bhnij,bhnjd->bhnid', T, (beta * jnp.exp(G))[..., None] * k)
    attn = jnp.einsum('bhnid,bhnjd->bhnij', q, k) * decay
    qg = q * jnp.exp(G)[..., None]
    G_last = G[..., -1]
    kd = k * jnp.exp(G_last[..., None] - G)[..., None]

    def step(S, xs):
        qg_c, w_c, u_c, attn_c, kd_c, gl_c = xs
        v_new = u_c - jnp.einsum('bhcd,bhde->bhce', w_c, S)
        o = jnp.einsum('bhcd,bhde->bhce', qg_c, S) + jnp.einsum('bhij,bhje->bhie', attn_c, v_new)
        S = S * jnp.exp(gl_c)[..., None, None] + jnp.einsum('bhcd,bhce->bhde', kd_c, v_new)
        return S, o

    xs = tuple(jnp.moveaxis(t, 2, 0) for t in (qg, w, u, attn, kd, G_last))
    S_fin, o = lax.scan(step, S0, xs)
    o = jnp.transpose(o, (1, 0, 3, 2, 4)).reshape(B, L, H, DV)
    return o, S_fin


def token_mixers(xn, prev_a, prev_qkv, S0, w_in, conv_a_w, w_out_a, conv_qkv_w,
                 a_log, dt_bias, onorm_g, w_out_b, w_o, chunk):
    Bsz, L, _ = xn.shape
    proj = xn @ w_in
    sizes = [D_CONV, D_CONV, D_CONV, QKV_DIM, V_DIM, N_HEADS, N_HEADS, D_MODEL, D_MODEL]
    offs = []
    acc = 0
    for s in sizes[:-1]:
        acc += s
        offs.append(acc)
    h, b_gate, c_gate, qkv, z, beta_raw, a_raw, gate_a, gate_b = jnp.split(proj, offs, axis=-1)

    conv_u, new_a = causal_conv(c_gate * h, prev_a, conv_a_w)
    y_a = (b_gate * conv_u) @ w_out_a

    qkv_c, new_qkv = causal_conv(qkv, prev_qkv, conv_qkv_w)
    qkv_c = jax.nn.silu(qkv_c).astype(jnp.float32)
    q, k, v = jnp.split(qkv_c, [QK_DIM, 2 * QK_DIM], axis=-1)
    q = l2norm(q.reshape(Bsz, L, N_HEADS, HEAD_K)) * (HEAD_K ** -0.5)
    k = l2norm(k.reshape(Bsz, L, N_HEADS, HEAD_K))
    v = v.reshape(Bsz, L, N_HEADS, HEAD_V)
    beta = jax.nn.sigmoid(beta_raw.astype(jnp.float32))
    g = -jnp.exp(a_log.astype(jnp.float32)) * jax.nn.softplus(
        a_raw.astype(jnp.float32) + dt_bias.astype(jnp.float32))
    o, S_fin = gated_delta_chunked(q, k, v, g, beta, S0.astype(jnp.float32), chunk)
    zf = z.astype(jnp.float32).reshape(Bsz, L, N_HEADS, HEAD_V)
    o = rmsnorm(o, onorm_g) * jax.nn.silu(zf)
    y_b = o.reshape(Bsz, L, V_DIM).astype(xn.dtype) @ w_out_b

    m = jax.nn.sigmoid(gate_a) * y_a + jax.nn.sigmoid(gate_b) * y_b
    return m @ w_o, new_a, new_qkv, S_fin.astype(xn.dtype)


def hier_moe(xn, w_rg, b_rg, w_re, b_re, w_gate, w_up, w_down):
    shp = xn.shape
    t = xn.reshape(-1, D_MODEL)
    lg = (t @ w_rg + b_rg).astype(jnp.float32)
    pg = jax.nn.softmax(lg, axis=-1)
    gsel = jnp.argmax(lg, axis=-1)
    pg_sel = jnp.take_along_axis(pg, gsel[:, None], axis=-1)[:, 0]
    le = (t @ w_re + b_re).astype(jnp.float32).reshape(-1, N_GROUPS, EXPERTS_PER_GROUP)
    le_sel = jnp.take_along_axis(le, gsel[:, None, None], axis=1)[:, 0]
    pe = jax.nn.softmax(le_sel, axis=-1)
    top_p, top_i = lax.top_k(pe, TOP_K)
    top_p = top_p / jnp.sum(top_p, axis=-1, keepdims=True)
    expert_id = gsel[:, None] * EXPERTS_PER_GROUP + top_i
    combine = jnp.sum(jax.nn.one_hot(expert_id, N_EXPERTS, dtype=jnp.float32)
                      * (pg_sel[:, None] * top_p)[..., None], axis=1)

    def step(acc, xs):
        wg, wu, wd, c = xs
        hdn = jax.nn.silu(t @ wg) * (t @ wu)
        return acc + c[:, None] * (hdn @ wd).astype(jnp.float32), None

    acc0 = jnp.zeros((t.shape[0], D_MODEL), jnp.float32)
    acc, _ = lax.scan(step, acc0, (w_gate, w_up, w_down, combine.T.astype(t.dtype)))
    return acc.astype(xn.dtype).reshape(shp)


def setup_inputs(seed: int = 0) -> dict:
    key = jax.random.key(seed)
    ks = jax.random.split(key, 24)
    nrm = lambda k, shape, scale: jax.random.normal(k, shape, jnp.float32) * scale
    x_prompt = nrm(ks[0], (BATCH, SEQ, D_MODEL), 1.0)
    x_sample = nrm(ks[1], (DEC_BATCH, DEC_SEQ, D_MODEL), 1.0)
    cache_conv_a = nrm(ks[2], (DEPTH, DEC_BATCH, CONV_A_WIDTH - 1, D_CONV), 1.0)
    cache_conv_qkv = nrm(ks[3], (DEPTH, DEC_BATCH, CONV_QKV_WIDTH - 1, QKV_DIM), 1.0)
    state_gdn = nrm(ks[4], (DEPTH, DEC_BATCH, N_HEADS, HEAD_K, HEAD_V), 0.3)
    norm1_g = 1.0 + nrm(ks[5], (DEPTH, D_MODEL), 0.02)
    w_in = nrm(ks[6], (DEPTH, D_MODEL, IN_DIM), D_MODEL ** -0.5)
    conv_a_w = nrm(ks[7], (DEPTH, CONV_A_WIDTH, D_CONV), CONV_A_WIDTH ** -0.5)
    w_out_a = nrm(ks[8], (DEPTH, D_CONV, D_MODEL), D_CONV ** -0.5)
    conv_qkv_w = nrm(ks[9], (DEPTH, CONV_QKV_WIDTH, QKV_DIM), CONV_QKV_WIDTH ** -0.5)
    a_log = jnp.log(jax.random.uniform(ks[10], (DEPTH, N_HEADS), jnp.float32, 1.0, 16.0))
    dt = jnp.exp(jax.random.uniform(ks[11], (DEPTH, N_HEADS), jnp.float32,
                                    math.log(1e-3), math.log(1e-1)))
    dt_bias = dt + jnp.log(-jnp.expm1(-dt))
    onorm_g = 1.0 + nrm(ks[12], (DEPTH, HEAD_V), 0.02)
    w_out_b = nrm(ks[13], (DEPTH, V_DIM, D_MODEL), V_DIM ** -0.5)
    w_o = nrm(ks[14], (DEPTH, D_MODEL, D_MODEL), D_MODEL ** -0.5)
    norm2_g = 1.0 + nrm(ks[15], (DEPTH, D_MODEL), 0.02)
    w_router_group = nrm(ks[16], (DEPTH, D_MODEL, N_GROUPS), D_MODEL ** -0.5)
    b_router_group = nrm(ks[17], (DEPTH, N_GROUPS), 0.01)
    w_router_expert = nrm(ks[18], (DEPTH, D_MODEL, N_EXPERTS), D_MODEL ** -0.5)
    b_router_expert = nrm(ks[19], (DEPTH, N_EXPERTS), 0.01)
    w_gate = nrm(ks[20], (DEPTH, N_EXPERTS, D_MODEL, D_EXPERT), D_MODEL ** -0.5)
    w_up = nrm(ks[21], (DEPTH, N_EXPERTS, D_MODEL, D_EXPERT), D_MODEL ** -0.5)
    w_down = nrm(ks[22], (DEPTH, N_EXPERTS, D_EXPERT, D_MODEL), D_EXPERT ** -0.5)
    final_g = 1.0 + nrm(ks[23], (D_MODEL,), 0.02)
    return {"x_prompt": x_prompt, "x_sample": x_sample,
            "cache_conv_a": cache_conv_a, "cache_conv_qkv": cache_conv_qkv, "state_gdn": state_gdn,
            "norm1_g": norm1_g, "w_in": w_in, "conv_a_w": conv_a_w, "w_out_a": w_out_a,
            "conv_qkv_w": conv_qkv_w, "a_log": a_log, "dt_bias": dt_bias, "onorm_g": onorm_g,
            "w_out_b": w_out_b, "w_o": w_o, "norm2_g": norm2_g,
            "w_router_group": w_router_group, "b_router_group": b_router_group,
            "w_router_expert": w_router_expert, "b_router_expert": b_router_expert,
            "w_gate": w_gate, "w_up": w_up, "w_down": w_down, "final_g": final_g}


def reference(x_prompt, x_sample, cache_conv_a, cache_conv_qkv, state_gdn,
              norm1_g, w_in, conv_a_w, w_out_a, conv_qkv_w, a_log, dt_bias, onorm_g,
              w_out_b, w_o, norm2_g, w_router_group, b_router_group,
              w_router_expert, b_router_expert, w_gate, w_up, w_down, final_g):
    def run(x, prev_a, prev_qkv, S0):
        chunk = min(CHUNK, x.shape[1])
        new_a, new_qkv, new_S = [], [], []
        for l in range(DEPTH):
            mix, na, nq, ns = token_mixers(rmsnorm(x, norm1_g[l]), prev_a[l], prev_qkv[l], S0[l],
                                           w_in[l], conv_a_w[l], w_out_a[l], conv_qkv_w[l],
                                           a_log[l], dt_bias[l], onorm_g[l], w_out_b[l], w_o[l], chunk)
            x = x + mix
            x = x + hier_moe(rmsnorm(x, norm2_g[l]), w_router_group[l], b_router_group[l],
                             w_router_expert[l], b_router_expert[l], w_gate[l], w_up[l], w_down[l])
            new_a.append(na)
            new_qkv.append(nq)
            new_S.append(ns)
        return rmsnorm(x, final_g), jnp.stack(new_a), jnp.stack(new_qkv), jnp.stack(new_S)

    bp = x_prompt.shape[0]
    dt_ = x_prompt.dtype
    zero_a = jnp.zeros((DEPTH, bp, CONV_A_WIDTH - 1, D_CONV), dt_)
    zero_qkv = jnp.zeros((DEPTH, bp, CONV_QKV_WIDTH - 1, QKV_DIM), dt_)
    zero_S = jnp.zeros((DEPTH, bp, N_HEADS, HEAD_K, HEAD_V), dt_)
    y_prompt, conv_a_prompt, conv_qkv_prompt, gdn_prompt = run(x_prompt, zero_a, zero_qkv, zero_S)
    y_sample, conv_a_sample, conv_qkv_sample, gdn_sample = run(x_sample, cache_conv_a,
                                                               cache_conv_qkv, state_gdn)
    return (y_prompt, y_sample, conv_a_prompt, conv_qkv_prompt, gdn_prompt,
            conv_a_sample, conv_qkv_sample, gdn_sample)
```

```python
import functools

import jax
import jax.numpy as jnp
from jax import lax
from jax.experimental import pallas as pl
from jax.experimental.pallas import tpu as pltpu

F32 = jnp.float32
BF16 = jnp.bfloat16
EPS = 1e-6
CHUNK = 64
LANES = 128
SUBLANES = 8
VMEM_LIMIT_BYTES = 56 * 1024 * 1024
TOKEN_TILE = 512
EXPERT_TILE = 256
ROW_TILE = 256


def _params(n_axes):
    return pltpu.CompilerParams(dimension_semantics=("arbitrary",) * n_axes,
                                vmem_limit_bytes=VMEM_LIMIT_BYTES)


def _resident(shape):
    return pl.BlockSpec(shape, lambda *_: (0,) * len(shape), pipeline_mode=pl.Buffered(1))


def _dot(a, b):
    return jnp.dot(a, b, preferred_element_type=F32)


def _sigmoid(x):
    return 1.0 / (1.0 + jnp.exp(-x))


def _inproj_body(x_ref, g_ref, w_ref, wab_ref, u_ref, b_ref, qkv_ref, z_ref, ga_ref, gb_ref, ab_ref, *, dc, dq):
    x = x_ref[...]
    r = lax.rsqrt(jnp.mean(x * x, axis=-1, keepdims=True) + EPS)
    xn = (x * r * g_ref[...]).astype(BF16)

    def mm(lo, n):
        return _dot(xn, w_ref[:, lo:lo + n])

    u_ref[...] = (mm(dc, dc) * mm(0, dc)).astype(BF16)
    b_ref[...] = mm(2 * dc, dc).astype(BF16)
    o = 3 * dc
    for j in range(dq // dc):
        qkv_ref[:, j * dc:(j + 1) * dc] = mm(o + j * dc, dc).astype(BF16)
    o += dq
    z_ref[...] = mm(o, z_ref.shape[1]).astype(BF16)
    o += z_ref.shape[1]
    ga_ref[...] = mm(o, ga_ref.shape[1]).astype(BF16)
    o += ga_ref.shape[1]
    gb_ref[...] = mm(o, gb_ref.shape[1]).astype(BF16)
    ab_ref[...] = _dot(xn, wab_ref[...])


def _inproj(x, g1, w_main, w_ab, *, dc, dq, dv, d):
    t = x.shape[0]
    tm = min(TOKEN_TILE, t)
    row = lambda n: pl.BlockSpec((tm, n), lambda i: (i, 0))
    out_shapes = (
        jax.ShapeDtypeStruct((t, dc), BF16), jax.ShapeDtypeStruct((t, dc), BF16),
        jax.ShapeDtypeStruct((t, dq), BF16), jax.ShapeDtypeStruct((t, dv), BF16),
        jax.ShapeDtypeStruct((t, d), BF16), jax.ShapeDtypeStruct((t, d), BF16),
        jax.ShapeDtypeStruct((t, LANES), F32))
    return pl.pallas_call(
        functools.partial(_inproj_body, dc=dc, dq=dq),
        grid=(t // tm,),
        in_specs=[row(d), _resident((1, d)), _resident(w_main.shape), _resident(w_ab.shape)],
        out_specs=(row(dc), row(dc), row(dq), row(dv), row(d), row(d), row(LANES)),
        out_shape=out_shapes,
        compiler_params=_params(1),
        name="inproj",
    )(x, g1, w_main, w_ab)


def _causal_conv(u, prev8, w_ref):
    width = w_ref.shape[0]
    n = u.shape[0]
    body = w_ref[width - 1:width, :] * u
    for sh in range(1, width):
        body = body + w_ref[width - 1 - sh:width - sh, :] * pltpu.roll(u, sh, 0)
    head_in = jnp.concatenate([prev8, u[0:SUBLANES]], axis=0)
    lo = SUBLANES - (width - 1)
    head = w_ref[0:1, :] * head_in[lo:lo + SUBLANES]
    for j in range(1, width):
        head = head + w_ref[j:j + 1, :] * head_in[lo + j:lo + j + SUBLANES]
    if n == SUBLANES:
        return head
    return jnp.concatenate([head, body[SUBLANES:]], axis=0)


def _mixer_a_body(u_ref, b_ref, ga_ref, cache_ref, cw_ref, wo_ref, ma_ref, carry_ref, v_ref, *, bs, tl):
    j = pl.program_id(1)
    for s in range(bs):
        u = u_ref[s].astype(F32)
        prev8 = jnp.where(j == 0, cache_ref[s], carry_ref[s])
        conv = _causal_conv(u, prev8, cw_ref)
        carry_ref[s] = u[tl - SUBLANES:tl]
        v_ref[s * tl:(s + 1) * tl, :] = (b_ref[s].astype(F32) * conv).astype(BF16)
    y = _dot(v_ref[...], wo_ref[...])
    for s in range(bs):
        ga = ga_ref[s].astype(F32)
        ma_ref[s] = (_sigmoid(ga) * y[s * tl:(s + 1) * tl]).astype(BF16)


def _seq_tiles(b, l):
    tl = min(l, TOKEN_TILE)
    bs = max(1, min(b, TOKEN_TILE // tl))
    while b % bs:
        bs -= 1
    return bs, tl


def _mixer_a(u, bg, ga, cache8, conv_w, w_out_a):
    b, l, dc = u.shape
    d = w_out_a.shape[1]
    bs, tl = _seq_tiles(b, l)
    blk = lambda n: pl.BlockSpec((bs, tl, n), lambda i, j: (i, j, 0))
    return pl.pallas_call(
        functools.partial(_mixer_a_body, bs=bs, tl=tl),
        grid=(b // bs, l // tl),
        in_specs=[blk(dc), blk(dc), blk(d),
                  pl.BlockSpec((bs, SUBLANES, dc), lambda i, j: (i, 0, 0)),
                  _resident(conv_w.shape), _resident(w_out_a.shape)],
        out_specs=blk(d),
        out_shape=jax.ShapeDtypeStruct((b, l, d), BF16),
        scratch_shapes=[pltpu.VMEM((bs, SUBLANES, dc), F32), pltpu.VMEM((bs * tl, dc), BF16)],
        compiler_params=_params(2),
        name="mixer_a",
    )(u, bg, ga, cache8, conv_w, w_out_a)


def _unit_lower_inverse(a):
    n = a.shape[0]
    r = lax.broadcasted_iota(jnp.int32, (n, n), 0)
    c = lax.broadcasted_iota(jnp.int32, (n, n), 1)
    eye = (r == c).astype(F32)

    def same_block(size):
        return jnp.bitwise_xor(r, c) < size

    bb = lambda m: m.astype(BF16)
    a8 = jnp.where(same_block(SUBLANES), a, 0.0)
    p2 = _dot(bb(a8), bb(a8))
    p4 = _dot(bb(p2), bb(p2))
    x = eye - a8
    x = x + _dot(bb(x), bb(p2))
    x = x + _dot(bb(x), bb(p4))
    size = SUBLANES
    while size < n:
        off = jnp.where(same_block(2 * size) & jnp.logical_not(same_block(size)), a, 0.0)
        y = _dot(bb(off), bb(x))
        x = x - _dot(bb(x), bb(y))
        size *= 2
    return x


def _gdn_body(qkv_ref, ab_ref, z_ref, cache_ref, s0_ref, cw_ref, alog_ref, dtb_ref, og_ref_g,
              og_ref, sfin_ref, carry_ref, s_ref, *, nh, dk, dv):
    j = pl.program_id(1)
    last = pl.num_programs(1) - 1
    n = CHUNK
    bb = lambda m: m.astype(BF16)

    @pl.when(j == 0)
    def _():
        s_ref[...] = s0_ref[0]

    x = qkv_ref[0].astype(F32)
    prev8 = jnp.where(j == 0, cache_ref[0], carry_ref[...])
    xc = _causal_conv(x, prev8, cw_ref)
    carry_ref[...] = x[n - SUBLANES:n]
    xc = xc * _sigmoid(xc)

    ab = ab_ref[0]
    beta = _sigmoid(ab)
    a_in = pltpu.roll(ab, LANES - nh, 1) + dtb_ref[...]
    softplus = jnp.maximum(a_in, 0.0) + jnp.log(1.0 + jnp.exp(-jnp.abs(a_in)))
    g = -jnp.exp(alog_ref[...]) * softplus
    rows = lax.broadcasted_iota(jnp.int32, (n, LANES), 0)
    gc = g
    sh = 1
    while sh < n:
        gc = gc + jnp.where(rows >= sh, pltpu.roll(gc, sh, 0), 0.0)
        sh *= 2
    gct = gc.T

    ri = lax.broadcasted_iota(jnp.int32, (n, n), 0)
    ci = lax.broadcasted_iota(jnp.int32, (n, n), 1)
    incl = ri >= ci
    strict = ri > ci

    for h in range(nh):
        q = xc[:, h * dk:(h + 1) * dk]
        k = xc[:, nh * dk + h * dk:nh * dk + (h + 1) * dk]
        v = xc[:, 2 * nh * dk + h * dv:2 * nh * dk + (h + 1) * dv]
        q = q * lax.rsqrt(jnp.sum(q * q, axis=-1, keepdims=True) + EPS) * (dk ** -0.5)
        k = k * lax.rsqrt(jnp.sum(k * k, axis=-1, keepdims=True) + EPS)
        bcol = beta[:, h:h + 1]
        gcol = gc[:, h:h + 1]
        grow = gct[h:h + 1, :]
        decay = jnp.where(incl, jnp.exp(jnp.where(incl, gcol - grow, 0.0)), 0.0)
        kb = bb(k)
        qk_kk = lax.dot_general(bb(jnp.concatenate([q, k], axis=0)), kb,
                                (((1,), (1,)), ((), ())), preferred_element_type=F32)
        attn = qk_kk[:n] * decay
        a = jnp.where(strict, bcol * qk_kk[n:] * decay, 0.0)
        t = _unit_lower_inverse(a)
        eg = jnp.exp(gcol)
        uw = _dot(bb(t), bb(jnp.concatenate([bcol * v, (bcol * eg) * k], axis=1)))
        u = uw[:, :dv]
        w = uw[:, dv:]
        s = s_ref[h]
        ws = _dot(bb(jnp.concatenate([w, q * eg], axis=0)), bb(s))
        v_new = u - ws[:n]
        o = ws[n:] + _dot(bb(attn), bb(v_new))
        gl = gc[n - 1:n, h:h + 1]
        kd = k * jnp.exp(gl - gcol)
        s_ref[h] = s * jnp.exp(gl) + _dot(bb(kd.T), bb(v_new))
        on = o * lax.rsqrt(jnp.mean(o * o, axis=-1, keepdims=True) + EPS) * og_ref_g[...]
        zz = z_ref[0, :, h * dv:(h + 1) * dv].astype(F32)
        og_ref[0, :, h * dv:(h + 1) * dv] = (on * (zz * _sigmoid(zz))).astype(BF16)

    @pl.when(j == last)
    def _():
        sfin_ref[0] = s_ref[...]


def _gdn(qkv, ab, z, cache8, s0, conv_w, alog, dtb, onorm_g, *, nh, dk, dv):
    b, l, dq = qkv.shape
    n = CHUNK
    blk = lambda c: pl.BlockSpec((1, n, c), lambda i, j: (i, j, 0))
    state = pl.BlockSpec((1, nh, dk, dv), lambda i, j: (i, 0, 0, 0))
    return pl.pallas_call(
        functools.partial(_gdn_body, nh=nh, dk=dk, dv=dv),
        grid=(b, l // n),
        in_specs=[blk(dq), blk(LANES), blk(nh * dv),
                  pl.BlockSpec((1, SUBLANES, dq), lambda i, j: (i, 0, 0)), state,
                  _resident(conv_w.shape), _resident(alog.shape), _resident(dtb.shape), _resident(onorm_g.shape)],
        out_specs=(blk(nh * dv), state),
        out_shape=(jax.ShapeDtypeStruct((b, l, nh * dv), BF16), jax.ShapeDtypeStruct((b, nh, dk, dv), F32)),
        scratch_shapes=[pltpu.VMEM((SUBLANES, dq), F32), pltpu.VMEM((nh, dk, dv), F32)],
        compiler_params=_params(2),
        name="gdn",
    )(qkv, ab, z, cache8, s0, conv_w, alog, dtb, onorm_g)


def _post_body(x_ref, og_ref, ma_ref, gb_ref, wob_ref, wo_ref, g2_ref, wr_ref, br_ref,
               x1_ref, t_ref, ri_ref, rw_ref, cnt_ref, base_ref, *, n_exp, n_grp, epg):
    i = pl.program_id(0)
    tm = x_ref.shape[0]

    @pl.when(i == 0)
    def _():
        base_ref[...] = jnp.zeros_like(base_ref)

    y_b = _dot(og_ref[...], wob_ref[...])
    m = ma_ref[...].astype(F32) + _sigmoid(gb_ref[...].astype(F32)) * y_b
    x1 = x_ref[...] + _dot(m.astype(BF16), wo_ref[...])
    x1_ref[...] = x1
    r = lax.rsqrt(jnp.mean(x1 * x1, axis=-1, keepdims=True) + EPS)
    t = x1 * r * g2_ref[...]
    t_ref[...] = t
    logits = _dot(t.astype(BF16), wr_ref[...]) + br_ref[...]
    lane = lax.broadcasted_iota(jnp.int32, (tm, LANES), 1)
    lane_f = lane.astype(F32)
    neg = -jnp.inf

    def first_max(v):
        mx = jnp.max(v, axis=-1, keepdims=True)
        idx = jnp.min(jnp.where(v == mx, lane_f, float(LANES)), axis=-1, keepdims=True)
        return mx, idx.astype(jnp.int32)

    gmask = (lane >= n_exp) & (lane < n_exp + n_grp)
    gmax, gidx = first_max(jnp.where(gmask, logits, neg))
    gsel = gidx - n_exp
    pg_sel = 1.0 / jnp.sum(jnp.where(gmask, jnp.exp(logits - gmax), 0.0), axis=-1, keepdims=True)
    emask = (lane >= gsel * epg) & (lane < (gsel + 1) * epg)
    le = jnp.where(emask, logits, neg)
    m1, e1 = first_max(le)
    m2, e2 = first_max(jnp.where(lane == e1, neg, le))
    ex = jnp.exp(m2 - m1)
    p1 = 1.0 / (1.0 + ex)
    p2 = ex * p1
    onehot = ((lane == e1) | (lane == e2)).astype(BF16)
    rr = lax.broadcasted_iota(jnp.int32, (tm, tm), 0)
    cc = lax.broadcasted_iota(jnp.int32, (tm, tm), 1)
    before = _dot((rr > cc).astype(BF16), onehot) + base_ref[0:1, :]
    rank1 = jnp.sum(jnp.where(lane == e1, before, 0.0), axis=-1, keepdims=True).astype(jnp.int32)
    rank2 = jnp.sum(jnp.where(lane == e2, before, 0.0), axis=-1, keepdims=True).astype(jnp.int32)
    base_ref[0:1, :] = base_ref[0:1, :] + jnp.sum(onehot.astype(F32), axis=0, keepdims=True)
    ri_ref[...] = jnp.where(lane == 0, e1, jnp.where(lane == 1, e2, jnp.where(lane == 2, rank1, jnp.where(lane == 3, rank2, 0))))
    rw_ref[...] = jnp.where(lane == 0, pg_sel * p1, jnp.where(lane == 1, pg_sel * p2, 0.0))
    cnt_ref[...] = base_ref[...]


def _post(x, og, ma, gb, w_out_b, w_o, g2, w_r, b_r, *, n_exp, n_grp):
    t, d = x.shape
    tm = min(TOKEN_TILE, t)
    row = lambda n: pl.BlockSpec((tm, n), lambda i: (i, 0))
    return pl.pallas_call(
        functools.partial(_post_body, n_exp=n_exp, n_grp=n_grp, epg=n_exp // n_grp),
        grid=(t // tm,),
        in_specs=[row(d), row(og.shape[1]), row(d), row(d), _resident(w_out_b.shape), _resident(w_o.shape),
                  _resident(g2.shape), _resident(w_r.shape), _resident(b_r.shape)],
        out_specs=(row(d), row(d), row(LANES), row(LANES), pl.BlockSpec((SUBLANES, LANES), lambda i: (0, 0))),
        out_shape=(jax.ShapeDtypeStruct((t, d), F32), jax.ShapeDtypeStruct((t, d), F32),
                   jax.ShapeDtypeStruct((t, LANES), jnp.int32), jax.ShapeDtypeStruct((t, LANES), F32),
                   jax.ShapeDtypeStruct((SUBLANES, LANES), F32)),
        scratch_shapes=[pltpu.VMEM((SUBLANES, LANES), F32)],
        compiler_params=_params(1),
        name="post",
    )(x, og, ma, gb, w_out_b, w_o, g2, w_r, b_r)


def _scatter_body(dest_ref, t_ref, xs_in_ref, xs_ref, sem, *, ts):
    del xs_in_ref

    def copy(r, k):
        return pltpu.make_async_copy(t_ref.at[pl.ds(r, 1)], xs_ref.at[pl.ds(dest_ref[0, 0, 2 * r + k], 1)], sem)

    def start(r, c):
        copy(r, 0).start()
        copy(r, 1).start()
        return c

    def wait(r, c):
        copy(r, 0).wait()
        copy(r, 1).wait()
        return c

    lax.fori_loop(0, ts, start, 0)
    lax.fori_loop(0, ts, wait, 0)


def _scatter(dest, t, xs_zero):
    n_tok, d = t.shape
    ts = min(ROW_TILE, n_tok)
    dest3 = dest.reshape(n_tok // ts, 1, 2 * ts)
    return pl.pallas_call(
        functools.partial(_scatter_body, ts=ts),
        grid=(n_tok // ts,),
        in_specs=[pl.BlockSpec((1, 1, 2 * ts), lambda i: (i, 0, 0), memory_space=pltpu.SMEM),
                  pl.BlockSpec((ts, d), lambda i: (i, 0)),
                  pl.BlockSpec(memory_space=pl.ANY)],
        out_specs=pl.BlockSpec(memory_space=pl.ANY),
        out_shape=jax.ShapeDtypeStruct(xs_zero.shape, xs_zero.dtype),
        scratch_shapes=[pltpu.SemaphoreType.DMA(())],
        input_output_aliases={2: 0},
        compiler_params=_params(1),
        name="moe_scatter",
    )(dest3, t, xs_zero)


def _experts_body(te_ref, nt_ref, xs_ref, wg_ref, wu_ref, wd_ref, ys_ref):
    i = pl.program_id(0)

    @pl.when(i < nt_ref[0])
    def _():
        x = xs_ref[...].astype(BF16)
        hg = _dot(x, wg_ref[...])
        hu = _dot(x, wu_ref[...])
        hdn = (hg * _sigmoid(hg)) * hu
        ys_ref[...] = _dot(hdn.astype(BF16), wd_ref[...])

    @pl.when(i >= nt_ref[0])
    def _():
        ys_ref[...] = jnp.zeros_like(ys_ref)


def _experts(tile_expert, n_tiles_used, xs, wg, wu, wd):
    rows, d = xs.shape
    r = EXPERT_TILE
    de = wg.shape[2]
    grid_spec = pltpu.PrefetchScalarGridSpec(
        num_scalar_prefetch=2,
        grid=(rows // r,),
        in_specs=[pl.BlockSpec((r, d), lambda i, te, nt: (i, 0)),
                  pl.BlockSpec((None, d, de), lambda i, te, nt: (te[i], 0, 0)),
                  pl.BlockSpec((None, d, de), lambda i, te, nt: (te[i], 0, 0)),
                  pl.BlockSpec((None, de, d), lambda i, te, nt: (te[i], 0, 0))],
        out_specs=pl.BlockSpec((r, d), lambda i, te, nt: (i, 0)),
    )
    return pl.pallas_call(
        _experts_body,
        grid_spec=grid_spec,
        out_shape=jax.ShapeDtypeStruct((rows, d), F32),
        compiler_params=_params(1),
        name="moe_experts",
    )(tile_expert, n_tiles_used, xs, wg, wu, wd)


def _combine_body(dest_ref, x1_ref, rw_ref, ys_ref, gf_ref, y_ref, buf_ref, sem, *, ts):
    def copy(r, k):
        return pltpu.make_async_copy(ys_ref.at[pl.ds(dest_ref[0, 0, 2 * r + k], 1)], buf_ref.at[k, pl.ds(r, 1)], sem)

    def start(r, c):
        copy(r, 0).start()
        copy(r, 1).start()
        return c

    def wait(r, c):
        copy(r, 0).wait()
        copy(r, 1).wait()
        return c

    lax.fori_loop(0, ts, start, 0)
    lax.fori_loop(0, ts, wait, 0)
    rw = rw_ref[...]
    acc = rw[:, 0:1] * buf_ref[0] + rw[:, 1:2] * buf_ref[1]
    x2 = x1_ref[...] + acc
    r = lax.rsqrt(jnp.mean(x2 * x2, axis=-1, keepdims=True) + EPS)
    y_ref[...] = x2 * r * gf_ref[...]


def _combine(dest, x1, rw, ys, gf):
    n_tok, d = x1.shape
    ts = min(ROW_TILE, n_tok)
    dest3 = dest.reshape(n_tok // ts, 1, 2 * ts)
    return pl.pallas_call(
        functools.partial(_combine_body, ts=ts),
        grid=(n_tok // ts,),
        in_specs=[pl.BlockSpec((1, 1, 2 * ts), lambda i: (i, 0, 0), memory_space=pltpu.SMEM),
                  pl.BlockSpec((ts, d), lambda i: (i, 0)),
                  pl.BlockSpec((ts, LANES), lambda i: (i, 0)),
                  pl.BlockSpec(memory_space=pl.ANY),
                  _resident(gf.shape)],
        out_specs=pl.BlockSpec((ts, d), lambda i: (i, 0)),
        out_shape=jax.ShapeDtypeStruct((n_tok, d), F32),
        scratch_shapes=[pltpu.VMEM((2, ts, d), F32), pltpu.SemaphoreType.DMA(())],
        compiler_params=_params(1),
        name="moe_combine",
    )(dest3, x1, rw, ys, gf)


def _moe(x1, t, ri, rw, counts, wg, wu, wd, gf, *, n_exp):
    n_tok, d = x1.shape
    r = EXPERT_TILE
    cnt = counts[0, :n_exp].astype(jnp.int32)
    padded = ((cnt + r - 1) // r) * r
    ends = jnp.cumsum(padded)
    offs = ends - padded
    n_tiles = (2 * n_tok) // r + n_exp
    tile_start = jnp.arange(n_tiles, dtype=jnp.int32) * r
    tile_expert = jnp.minimum(jnp.sum(tile_start[:, None] >= ends[None, :], axis=1), n_exp - 1).astype(jnp.int32)
    n_used = (ends[-1] // r).astype(jnp.int32).reshape(1)
    dest = jnp.take(offs, ri[:, 0:2], axis=0) + ri[:, 2:4]
    xs = _scatter(dest, t, jnp.zeros((n_tiles * r, d), F32))
    ys = _experts(tile_expert, n_used, xs, wg, wu, wd)
    return _combine(dest, x1, rw, ys, gf)


def _pad_rows8(c):
    return jnp.pad(c, ((0, 0), (SUBLANES - c.shape[1], 0), (0, 0)))


def _lane_row(v):
    return jnp.pad(v.astype(F32), (0, LANES - v.shape[0])).reshape(1, LANES)


def _run(x, prev_a, prev_qkv, s0, p, dims):
    b, l, d = x.shape
    dc, dq, dv, nh, dk, dvh, n_exp, n_grp = dims
    xf = x.reshape(b * l, d)
    u, bg, qkv, z, ga, gb, ab = _inproj(xf, p["g1"], p["w_main"], p["w_ab"], dc=dc, dq=dq, dv=dv, d=d)
    r3 = lambda a: a.reshape(b, l, a.shape[-1])
    ma = _mixer_a(r3(u), r3(bg), r3(ga), _pad_rows8(prev_a), p["conv_a_w"], p["w_out_a"])
    og, s_fin = _gdn(r3(qkv), r3(ab), r3(z), _pad_rows8(prev_qkv), s0, p["conv_qkv_w"], p["alog"], p["dtb"],
                     p["onorm_g"], nh=nh, dk=dk, dv=dvh)
    x1, t, ri, rw, counts = _post(xf, og.reshape(b * l, dv), ma.reshape(b * l, d), gb, p["w_out_b"], p["w_o"],
                                  p["g2"], p["w_r"], p["b_r"], n_exp=n_exp, n_grp=n_grp)
    y = _moe(x1, t, ri, rw, counts, p["wg"], p["wu"], p["wd"], p["gf"], n_exp=n_exp)
    wa = prev_a.shape[1]
    wq = prev_qkv.shape[1]
    new_a = r3(u)[:, l - wa:, :].astype(F32)
    new_qkv = r3(qkv)[:, l - wq:, :].astype(F32)
    return y.reshape(b, l, d), new_a, new_qkv, s_fin


def kernel(x_prompt, x_sample, cache_conv_a, cache_conv_qkv, state_gdn, norm1_g, w_in, conv_a_w, w_out_a, conv_qkv_w, a_log, dt_bias, onorm_g, w_out_b, w_o, norm2_g, w_router_group, b_router_group, w_router_expert, b_router_expert, w_gate, w_up, w_down, final_g):
    depth = w_in.shape[0]
    assert depth == 1, "single trunk layer"
    d = x_prompt.shape[-1]
    dc = conv_a_w.shape[-1]
    dq = conv_qkv_w.shape[-1]
    nh = a_log.shape[-1]
    dk, dvh = state_gdn.shape[-2], state_gdn.shape[-1]
    dv = nh * dvh
    n_grp = w_router_group.shape[-1]
    n_exp = w_router_expert.shape[-1]
    dims = (dc, dq, dv, nh, dk, dvh, n_exp, n_grp)

    w = w_in[0]
    o = [0, dc, 2 * dc, 3 * dc, 3 * dc + dq, 3 * dc + dq + dv, 3 * dc + dq + dv + nh, 3 * dc + dq + dv + 2 * nh,
         3 * dc + dq + dv + 2 * nh + d, 3 * dc + dq + dv + 2 * nh + 2 * d]
    seg = lambda k: w[:, o[k]:o[k + 1]]
    w_main = jnp.concatenate([seg(0), seg(2), seg(1), seg(3), seg(4), seg(7), seg(8)], axis=1).astype(BF16)
    w_ab = jnp.pad(jnp.concatenate([seg(5), seg(6)], axis=1), ((0, 0), (0, LANES - 2 * nh))).astype(BF16)
    w_r = jnp.pad(jnp.concatenate([w_router_expert[0], w_router_group[0]], axis=1),
                  ((0, 0), (0, LANES - n_exp - n_grp))).astype(BF16)
    b_r = _lane_row(jnp.concatenate([b_router_expert[0], b_router_group[0]]))
    p = dict(
        g1=norm1_g[0].reshape(1, d), w_main=w_main, w_ab=w_ab,
        conv_a_w=conv_a_w[0], w_out_a=w_out_a[0].astype(BF16),
        conv_qkv_w=conv_qkv_w[0], alog=_lane_row(a_log[0]), dtb=_lane_row(dt_bias[0]),
        onorm_g=onorm_g[0].reshape(1, dvh),
        w_out_b=w_out_b[0].astype(BF16), w_o=w_o[0].astype(BF16), g2=norm2_g[0].reshape(1, d),
        w_r=w_r, b_r=b_r,
        wg=w_gate[0].astype(BF16), wu=w_up[0].astype(BF16), wd=w_down[0].astype(BF16),
        gf=final_g.reshape(1, d),
    )
    bp = x_prompt.shape[0]
    dt_ = x_prompt.dtype
    zero_a = jnp.zeros((bp,) + cache_conv_a.shape[2:], dt_)
    zero_qkv = jnp.zeros((bp,) + cache_conv_qkv.shape[2:], dt_)
    zero_s = jnp.zeros((bp,) + state_gdn.shape[2:], dt_)
    y_p, a_p, q_p, s_p = _run(x_prompt, zero_a, zero_qkv, zero_s, p, dims)
    y_s, a_s, q_s, s_s = _run(x_sample, cache_conv_a[0], cache_conv_qkv[0], state_gdn[0], p, dims)
    return (y_p, y_s, a_p[None], q_p[None], s_p[None], a_s[None], q_s[None], s_s[None])
```

```python
import functools

import jax
import jax.numpy as jnp
from jax import lax
from jax.experimental import pallas as pl
from jax.experimental.pallas import tpu as pltpu

F32 = jnp.float32
BF16 = jnp.bfloat16
EPS = 1e-6
CHUNK = 64
LANES = 128
SUBLANES = 8
VMEM_LIMIT_BYTES = 56 * 1024 * 1024
TOKEN_TILE = 512
EXPERT_TILE = 256
ROW_TILE = 256


def _params(n_axes):
    return pltpu.CompilerParams(dimension_semantics=("arbitrary",) * n_axes,
                                vmem_limit_bytes=VMEM_LIMIT_BYTES)


def _resident(shape):
    return pl.BlockSpec(shape, lambda *_: (0,) * len(shape), pipeline_mode=pl.Buffered(1))


def _dot(a, b):
    return jnp.dot(a, b, preferred_element_type=F32)


def _sigmoid(x):
    return 1.0 / (1.0 + jnp.exp(-x))


def _inproj_body(x_ref, g_ref, w_ref, wab_ref, u_ref, b_ref, qkv_ref, z_ref, ga_ref, gb_ref, ab_ref, *, dc, dq):
    x = x_ref[...]
    r = lax.rsqrt(jnp.mean(x * x, axis=-1, keepdims=True) + EPS)
    xn = (x * r * g_ref[...]).astype(BF16)

    def mm(lo, n):
        return _dot(xn, w_ref[:, lo:lo + n])

    u_ref[...] = (mm(dc, dc) * mm(0, dc)).astype(BF16)
    b_ref[...] = mm(2 * dc, dc).astype(BF16)
    o = 3 * dc
    for j in range(dq // dc):
        qkv_ref[:, j * dc:(j + 1) * dc] = mm(o + j * dc, dc).astype(BF16)
    o += dq
    z_ref[...] = mm(o, z_ref.shape[1]).astype(BF16)
    o += z_ref.shape[1]
    ga_ref[...] = mm(o, ga_ref.shape[1]).astype(BF16)
    o += ga_ref.shape[1]
    gb_ref[...] = mm(o, gb_ref.shape[1]).astype(BF16)
    ab_ref[...] = _dot(xn, wab_ref[...])


def _inproj(x, g1, w_main, w_ab, *, dc, dq, dv, d):
    t = x.shape[0]
    tm = min(TOKEN_TILE, t)
    row = lambda n: pl.BlockSpec((tm, n), lambda i: (i, 0))
    out_shapes = (
        jax.ShapeDtypeStruct((t, dc), BF16), jax.ShapeDtypeStruct((t, dc), BF16),
        jax.ShapeDtypeStruct((t, dq), BF16), jax.ShapeDtypeStruct((t, dv), BF16),
        jax.ShapeDtypeStruct((t, d), BF16), jax.ShapeDtypeStruct((t, d), BF16),
        jax.ShapeDtypeStruct((t, LANES), F32))
    return pl.pallas_call(
        functools.partial(_inproj_body, dc=dc, dq=dq),
        grid=(t // tm,),
        in_specs=[row(d), _resident((1, d)), _resident(w_main.shape), _resident(w_ab.shape)],
        out_specs=(row(dc), row(dc), row(dq), row(dv), row(d), row(d), row(LANES)),
        out_shape=out_shapes,
        compiler_params=_params(1),
        name="inproj",
    )(x, g1, w_main, w_ab)


def _causal_conv(u, prev8, w_ref):
    width = w_ref.shape[0]
    n = u.shape[0]
    body = w_ref[width - 1:width, :] * u
    for sh in range(1, width):
        body = body + w_ref[width - 1 - sh:width - sh, :] * pltpu.roll(u, sh, 0)
    head_in = jnp.concatenate([prev8, u[0:SUBLANES]], axis=0)
    lo = SUBLANES - (width - 1)
    head = w_ref[0:1, :] * head_in[lo:lo + SUBLANES]
    for j in range(1, width):
        head = head + w_ref[j:j + 1, :] * head_in[lo + j:lo + j + SUBLANES]
    if n == SUBLANES:
        return head
    return jnp.concatenate([head, body[SUBLANES:]], axis=0)


def _mixer_a_body(u_ref, b_ref, ga_ref, cache_ref, cw_ref, wo_ref, ma_ref, carry_ref, v_ref, *, bs, tl):
    j = pl.program_id(1)
    for s in range(bs):
        u = u_ref[s].astype(F32)
        prev8 = jnp.where(j == 0, cache_ref[s], carry_ref[s])
        conv = _causal_conv(u, prev8, cw_ref)
        carry_ref[s] = u[tl - SUBLANES:tl]
        v_ref[s * tl:(s + 1) * tl, :] = (b_ref[s].astype(F32) * conv).astype(BF16)
    y = _dot(v_ref[...], wo_ref[...])
    for s in range(bs):
        ga = ga_ref[s].astype(F32)
        ma_ref[s] = (_sigmoid(ga) * y[s * tl:(s + 1) * tl]).astype(BF16)


def _seq_tiles(b, l):
    tl = min(l, TOKEN_TILE)
    bs = max(1, min(b, TOKEN_TILE // tl))
    while b % bs:
        bs -= 1
    return bs, tl


def _mixer_a(u, bg, ga, cache8, conv_w, w_out_a):
    b, l, dc = u.shape
    d = w_out_a.shape[1]
    bs, tl = _seq_tiles(b, l)
    blk = lambda n: pl.BlockSpec((bs, tl, n), lambda i, j: (i, j, 0))
    return pl.pallas_call(
        functools.partial(_mixer_a_body, bs=bs, tl=tl),
        grid=(b // bs, l // tl),
        in_specs=[blk(dc), blk(dc), blk(d),
                  pl.BlockSpec((bs, SUBLANES, dc), lambda i, j: (i, 0, 0)),
                  _resident(conv_w.shape), _resident(w_out_a.shape)],
        out_specs=blk(d),
        out_shape=jax.ShapeDtypeStruct((b, l, d), BF16),
        scratch_shapes=[pltpu.VMEM((bs, SUBLANES, dc), F32), pltpu.VMEM((bs * tl, dc), BF16)],
        compiler_params=_params(2),
        name="mixer_a",
    )(u, bg, ga, cache8, conv_w, w_out_a)


def _unit_lower_inverse(mats):
    n = mats[0].shape[0]
    r = lax.broadcasted_iota(jnp.int32, (n, n), 0)
    c = lax.broadcasted_iota(jnp.int32, (n, n), 1)
    eye = (r == c).astype(F32)

    def same_block(size):
        return jnp.bitwise_xor(r, c) < size

    bb = lambda m: m.astype(BF16)
    a8 = [jnp.where(same_block(SUBLANES), a, 0.0) for a in mats]
    p2 = [bb(_dot(bb(a), bb(a))) for a in a8]
    p4 = [bb(_dot(p, p)) for p in p2]
    xs = [eye - a for a in a8]
    xs = [x + _dot(bb(x), p) for x, p in zip(xs, p2)]
    xs = [x + _dot(bb(x), p) for x, p in zip(xs, p4)]
    size = SUBLANES
    while size < n:
        in_pair = same_block(2 * size) & jnp.logical_not(same_block(size))
        ys = [_dot(bb(jnp.where(in_pair, a, 0.0)), bb(x)) for a, x in zip(mats, xs)]
        xs = [x - _dot(bb(x), bb(y)) for x, y in zip(xs, ys)]
        size *= 2
    return xs


def _gdn_body(qkv_ref, ab_ref, z_ref, cache_ref, s0_ref, cw_ref, alog_ref, dtb_ref, og_ref_g,
              og_ref, sfin_ref, carry_ref, s_ref, *, nb, nh, dk, dv):
    j = pl.program_id(1)
    last = pl.num_programs(1) - 1
    n = CHUNK
    bb = lambda m: m.astype(BF16)

    @pl.when(j == 0)
    def _():
        s_ref[...] = s0_ref[...]

    rows = lax.broadcasted_iota(jnp.int32, (n, LANES), 0)
    ri = lax.broadcasted_iota(jnp.int32, (n, n), 0)
    ci = lax.broadcasted_iota(jnp.int32, (n, n), 1)
    incl = ri >= ci
    strict = ri > ci

    qs, ks, vs, bcols, gcols, gls, decays = [], [], [], [], [], [], []
    for s in range(nb):
        x = qkv_ref[s].astype(F32)
        prev8 = jnp.where(j == 0, cache_ref[s], carry_ref[s])
        xc = _causal_conv(x, prev8, cw_ref)
        carry_ref[s] = x[n - SUBLANES:n]
        xc = xc * _sigmoid(xc)
        ab = ab_ref[s]
        beta = _sigmoid(ab)
        a_in = pltpu.roll(ab, LANES - nh, 1) + dtb_ref[...]
        softplus = jnp.maximum(a_in, 0.0) + jnp.log(1.0 + jnp.exp(-jnp.abs(a_in)))
        gc = -jnp.exp(alog_ref[...]) * softplus
        sh = 1
        while sh < n:
            gc = gc + jnp.where(rows >= sh, pltpu.roll(gc, sh, 0), 0.0)
            sh *= 2
        gct = gc.T
        for h in range(nh):
            q = xc[:, h * dk:(h + 1) * dk]
            k = xc[:, nh * dk + h * dk:nh * dk + (h + 1) * dk]
            qs.append(q * lax.rsqrt(jnp.sum(q * q, axis=-1, keepdims=True) + EPS) * (dk ** -0.5))
            ks.append(k * lax.rsqrt(jnp.sum(k * k, axis=-1, keepdims=True) + EPS))
            vs.append(xc[:, 2 * nh * dk + h * dv:2 * nh * dk + (h + 1) * dv])
            bcols.append(beta[:, h:h + 1])
            gcol = gc[:, h:h + 1]
            gcols.append(gcol)
            gls.append(gc[n - 1:n, h:h + 1])
            decays.append(jnp.where(incl, jnp.exp(jnp.where(incl, gcol - gct[h:h + 1, :], 0.0)), 0.0))

    np_ = nb * nh
    qk_kk = [lax.dot_general(bb(jnp.concatenate([qs[p], ks[p]], axis=0)), bb(ks[p]),
                             (((1,), (1,)), ((), ())), preferred_element_type=F32) for p in range(np_)]
    attn = [bb(qk_kk[p][:n] * decays[p]) for p in range(np_)]
    ts = _unit_lower_inverse([jnp.where(strict, bcols[p] * qk_kk[p][n:] * decays[p], 0.0) for p in range(np_)])
    egs = [jnp.exp(g) for g in gcols]
    uw = [_dot(bb(ts[p]), bb(jnp.concatenate([bcols[p] * vs[p], (bcols[p] * egs[p]) * ks[p]], axis=1)))
          for p in range(np_)]
    kdt = [bb((ks[p] * jnp.exp(gls[p] - gcols[p])).T) for p in range(np_)]

    sidx = [(p // nh, p % nh) for p in range(np_)]
    s_old = [s_ref[a, h] for a, h in sidx]
    ws = [_dot(bb(jnp.concatenate([uw[p][:, dv:], qs[p] * egs[p]], axis=0)), bb(s_old[p])) for p in range(np_)]
    v_new = [bb(uw[p][:, :dv] - ws[p][:n]) for p in range(np_)]
    o_in = [_dot(attn[p], v_new[p]) for p in range(np_)]
    ds = [_dot(kdt[p], v_new[p]) for p in range(np_)]
    for p, (a, h) in enumerate(sidx):
        s_ref[a, h] = s_old[p] * jnp.exp(gls[p]) + ds[p]
        o = ws[p][n:] + o_in[p]
        on = o * lax.rsqrt(jnp.mean(o * o, axis=-1, keepdims=True) + EPS) * og_ref_g[...]
        zz = z_ref[a, :, h * dv:(h + 1) * dv].astype(F32)
        og_ref[a, :, h * dv:(h + 1) * dv] = (on * (zz * _sigmoid(zz))).astype(BF16)

    @pl.when(j == last)
    def _():
        sfin_ref[...] = s_ref[...]


GDN_SEQS_PER_STEP = 4


def _gdn(qkv, ab, z, cache8, s0, conv_w, alog, dtb, onorm_g, *, nh, dk, dv):
    b, l, dq = qkv.shape
    n = CHUNK
    nb = GDN_SEQS_PER_STEP if b % GDN_SEQS_PER_STEP == 0 else 1
    blk = lambda c: pl.BlockSpec((nb, n, c), lambda i, j: (i, j, 0))
    state = pl.BlockSpec((nb, nh, dk, dv), lambda i, j: (i, 0, 0, 0))
    return pl.pallas_call(
        functools.partial(_gdn_body, nb=nb, nh=nh, dk=dk, dv=dv),
        grid=(b // nb, l // n),
        in_specs=[blk(dq), blk(LANES), blk(nh * dv),
                  pl.BlockSpec((nb, SUBLANES, dq), lambda i, j: (i, 0, 0)), state,
                  _resident(conv_w.shape), _resident(alog.shape), _resident(dtb.shape), _resident(onorm_g.shape)],
        out_specs=(blk(nh * dv), state),
        out_shape=(jax.ShapeDtypeStruct((b, l, nh * dv), BF16), jax.ShapeDtypeStruct((b, nh, dk, dv), F32)),
        scratch_shapes=[pltpu.VMEM((nb, SUBLANES, dq), F32), pltpu.VMEM((nb, nh, dk, dv), F32)],
        compiler_params=_params(2),
        name="gdn",
    )(qkv, ab, z, cache8, s0, conv_w, alog, dtb, onorm_g)


def _post_body(x_ref, og_ref, ma_ref, gb_ref, wob_ref, wo_ref, g2_ref, wr_ref, br_ref,
               x1_ref, t_ref, ri_ref, rw_ref, cnt_ref, base_ref, *, n_exp, n_grp, epg):
    i = pl.program_id(0)
    tm = x_ref.shape[0]

    @pl.when(i == 0)
    def _():
        base_ref[...] = jnp.zeros_like(base_ref)

    y_b = _dot(og_ref[...], wob_ref[...])
    m = ma_ref[...].astype(F32) + _sigmoid(gb_ref[...].astype(F32)) * y_b
    x1 = x_ref[...] + _dot(m.astype(BF16), wo_ref[...])
    x1_ref[...] = x1
    r = lax.rsqrt(jnp.mean(x1 * x1, axis=-1, keepdims=True) + EPS)
    t = x1 * r * g2_ref[...]
    t_ref[...] = t
    logits = _dot(t.astype(BF16), wr_ref[...]) + br_ref[...]
    lane = lax.broadcasted_iota(jnp.int32, (tm, LANES), 1)
    lane_f = lane.astype(F32)
    neg = -jnp.inf

    def first_max(v):
        mx = jnp.max(v, axis=-1, keepdims=True)
        idx = jnp.min(jnp.where(v == mx, lane_f, float(LANES)), axis=-1, keepdims=True)
        return mx, idx.astype(jnp.int32)

    gmask = (lane >= n_exp) & (lane < n_exp + n_grp)
    gmax, gidx = first_max(jnp.where(gmask, logits, neg))
    gsel = gidx - n_exp
    pg_sel = 1.0 / jnp.sum(jnp.where(gmask, jnp.exp(logits - gmax), 0.0), axis=-1, keepdims=True)
    emask = (lane >= gsel * epg) & (lane < (gsel + 1) * epg)
    le = jnp.where(emask, logits, neg)
    m1, e1 = first_max(le)
    m2, e2 = first_max(jnp.where(lane == e1, neg, le))
    ex = jnp.exp(m2 - m1)
    p1 = 1.0 / (1.0 + ex)
    p2 = ex * p1
    onehot = ((lane == e1) | (lane == e2)).astype(BF16)
    rr = lax.broadcasted_iota(jnp.int32, (tm, tm), 0)
    cc = lax.broadcasted_iota(jnp.int32, (tm, tm), 1)
    before = _dot((rr > cc).astype(BF16), onehot) + base_ref[0:1, :]
    rank1 = jnp.sum(jnp.where(lane == e1, before, 0.0), axis=-1, keepdims=True).astype(jnp.int32)
    rank2 = jnp.sum(jnp.where(lane == e2, before, 0.0), axis=-1, keepdims=True).astype(jnp.int32)
    base_ref[0:1, :] = base_ref[0:1, :] + jnp.sum(onehot.astype(F32), axis=0, keepdims=True)
    ri_ref[...] = jnp.where(lane == 0, e1, jnp.where(lane == 1, e2, jnp.where(lane == 2, rank1, jnp.where(lane == 3, rank2, 0))))
    rw_ref[...] = jnp.where(lane == 0, pg_sel * p1, jnp.where(lane == 1, pg_sel * p2, 0.0))
    cnt_ref[...] = base_ref[...]


def _post(x, og, ma, gb, w_out_b, w_o, g2, w_r, b_r, *, n_exp, n_grp):
    t, d = x.shape
    tm = min(TOKEN_TILE, t)
    row = lambda n: pl.BlockSpec((tm, n), lambda i: (i, 0))
    return pl.pallas_call(
        functools.partial(_post_body, n_exp=n_exp, n_grp=n_grp, epg=n_exp // n_grp),
        grid=(t // tm,),
        in_specs=[row(d), row(og.shape[1]), row(d), row(d), _resident(w_out_b.shape), _resident(w_o.shape),
                  _resident(g2.shape), _resident(w_r.shape), _resident(b_r.shape)],
        out_specs=(row(d), row(d), row(LANES), row(LANES), pl.BlockSpec((SUBLANES, LANES), lambda i: (0, 0))),
        out_shape=(jax.ShapeDtypeStruct((t, d), F32), jax.ShapeDtypeStruct((t, d), F32),
                   jax.ShapeDtypeStruct((t, LANES), jnp.int32), jax.ShapeDtypeStruct((t, LANES), F32),
                   jax.ShapeDtypeStruct((SUBLANES, LANES), F32)),
        scratch_shapes=[pltpu.VMEM((SUBLANES, LANES), F32)],
        compiler_params=_params(1),
        name="post",
    )(x, og, ma, gb, w_out_b, w_o, g2, w_r, b_r)


def _scatter_body(dest_ref, t_ref, xs_in_ref, xs_ref, sem, *, ts):
    del xs_in_ref

    def copy(r, k):
        return pltpu.make_async_copy(t_ref.at[pl.ds(r, 1)], xs_ref.at[pl.ds(dest_ref[0, 0, 2 * r + k], 1)], sem)

    def start(r, c):
        copy(r, 0).start()
        copy(r, 1).start()
        return c

    def wait(r, c):
        copy(r, 0).wait()
        copy(r, 1).wait()
        return c

    lax.fori_loop(0, ts, start, 0)
    lax.fori_loop(0, ts, wait, 0)


def _scatter(dest, t, xs_zero):
    n_tok, d = t.shape
    ts = min(ROW_TILE, n_tok)
    dest3 = dest.reshape(n_tok // ts, 1, 2 * ts)
    return pl.pallas_call(
        functools.partial(_scatter_body, ts=ts),
        grid=(n_tok // ts,),
        in_specs=[pl.BlockSpec((1, 1, 2 * ts), lambda i: (i, 0, 0), memory_space=pltpu.SMEM),
                  pl.BlockSpec((ts, d), lambda i: (i, 0)),
                  pl.BlockSpec(memory_space=pl.ANY)],
        out_specs=pl.BlockSpec(memory_space=pl.ANY),
        out_shape=jax.ShapeDtypeStruct(xs_zero.shape, xs_zero.dtype),
        scratch_shapes=[pltpu.SemaphoreType.DMA(())],
        input_output_aliases={2: 0},
        compiler_params=_params(1),
        name="moe_scatter",
    )(dest3, t, xs_zero)


def _experts_body(te_ref, nt_ref, xs_ref, wg_ref, wu_ref, wd_ref, ys_ref):
    i = pl.program_id(0)

    @pl.when(i < nt_ref[0])
    def _():
        x = xs_ref[...].astype(BF16)
        hg = _dot(x, wg_ref[...])
        hu = _dot(x, wu_ref[...])
        hdn = (hg * _sigmoid(hg)) * hu
        ys_ref[...] = _dot(hdn.astype(BF16), wd_ref[...])

    @pl.when(i >= nt_ref[0])
    def _():
        ys_ref[...] = jnp.zeros_like(ys_ref)


def _experts(tile_expert, n_tiles_used, xs, wg, wu, wd):
    rows, d = xs.shape
    r = EXPERT_TILE
    de = wg.shape[2]
    grid_spec = pltpu.PrefetchScalarGridSpec(
        num_scalar_prefetch=2,
        grid=(rows // r,),
        in_specs=[pl.BlockSpec((r, d), lambda i, te, nt: (i, 0)),
                  pl.BlockSpec((None, d, de), lambda i, te, nt: (te[i], 0, 0)),
                  pl.BlockSpec((None, d, de), lambda i, te, nt: (te[i], 0, 0)),
                  pl.BlockSpec((None, de, d), lambda i, te, nt: (te[i], 0, 0))],
        out_specs=pl.BlockSpec((r, d), lambda i, te, nt: (i, 0)),
    )
    return pl.pallas_call(
        _experts_body,
        grid_spec=grid_spec,
        out_shape=jax.ShapeDtypeStruct((rows, d), F32),
        compiler_params=_params(1),
        name="moe_experts",
    )(tile_expert, n_tiles_used, xs, wg, wu, wd)


def _combine_body(dest_ref, x1_ref, rw_ref, ys_ref, gf_ref, y_ref, buf_ref, sem, *, ts):
    def copy(r, k):
        return pltpu.make_async_copy(ys_ref.at[pl.ds(dest_ref[0, 0, 2 * r + k], 1)], buf_ref.at[k, pl.ds(r, 1)], sem)

    def start(r, c):
        copy(r, 0).start()
        copy(r, 1).start()
        return c

    def wait(r, c):
        copy(r, 0).wait()
        copy(r, 1).wait()
        return c

    lax.fori_loop(0, ts, start, 0)
    lax.fori_loop(0, ts, wait, 0)
    rw = rw_ref[...]
    acc = rw[:, 0:1] * buf_ref[0] + rw[:, 1:2] * buf_ref[1]
    x2 = x1_ref[...] + acc
    r = lax.rsqrt(jnp.mean(x2 * x2, axis=-1, keepdims=True) + EPS)
    y_ref[...] = x2 * r * gf_ref[...]


def _combine(dest, x1, rw, ys, gf):
    n_tok, d = x1.shape
    ts = min(ROW_TILE, n_tok)
    dest3 = dest.reshape(n_tok // ts, 1, 2 * ts)
    return pl.pallas_call(
        functools.partial(_combine_body, ts=ts),
        grid=(n_tok // ts,),
        in_specs=[pl.BlockSpec((1, 1, 2 * ts), lambda i: (i, 0, 0), memory_space=pltpu.SMEM),
                  pl.BlockSpec((ts, d), lambda i: (i, 0)),
                  pl.BlockSpec((ts, LANES), lambda i: (i, 0)),
                  pl.BlockSpec(memory_space=pl.ANY),
                  _resident(gf.shape)],
        out_specs=pl.BlockSpec((ts, d), lambda i: (i, 0)),
        out_shape=jax.ShapeDtypeStruct((n_tok, d), F32),
        scratch_shapes=[pltpu.VMEM((2, ts, d), F32), pltpu.SemaphoreType.DMA(())],
        compiler_params=_params(1),
        name="moe_combine",
    )(dest3, x1, rw, ys, gf)


def _moe(x1, t, ri, rw, counts, wg, wu, wd, gf, *, n_exp):
    n_tok, d = x1.shape
    r = EXPERT_TILE
    cnt = counts[0, :n_exp].astype(jnp.int32)
    padded = ((cnt + r - 1) // r) * r
    ends = jnp.cumsum(padded)
    offs = ends - padded
    n_tiles = (2 * n_tok) // r + n_exp
    tile_start = jnp.arange(n_tiles, dtype=jnp.int32) * r
    tile_expert = jnp.minimum(jnp.sum(tile_start[:, None] >= ends[None, :], axis=1), n_exp - 1).astype(jnp.int32)
    n_used = (ends[-1] // r).astype(jnp.int32).reshape(1)
    dest = jnp.take(offs, ri[:, 0:2], axis=0) + ri[:, 2:4]
    xs = _scatter(dest, t, jnp.zeros((n_tiles * r, d), F32))
    ys = _experts(tile_expert, n_used, xs, wg, wu, wd)
    return _combine(dest, x1, rw, ys, gf)


def _pad_rows8(c):
    return jnp.pad(c, ((0, 0), (SUBLANES - c.shape[1], 0), (0, 0)))


def _lane_row(v):
    return jnp.pad(v.astype(F32), (0, LANES - v.shape[0])).reshape(1, LANES)


def _run(x, prev_a, prev_qkv, s0, p, dims):
    b, l, d = x.shape
    dc, dq, dv, nh, dk, dvh, n_exp, n_grp = dims
    xf = x.reshape(b * l, d)
    u, bg, qkv, z, ga, gb, ab = _inproj(xf, p["g1"], p["w_main"], p["w_ab"], dc=dc, dq=dq, dv=dv, d=d)
    r3 = lambda a: a.reshape(b, l, a.shape[-1])
    ma = _mixer_a(r3(u), r3(bg), r3(ga), _pad_rows8(prev_a), p["conv_a_w"], p["w_out_a"])
    og, s_fin = _gdn(r3(qkv), r3(ab), r3(z), _pad_rows8(prev_qkv), s0, p["conv_qkv_w"], p["alog"], p["dtb"],
                     p["onorm_g"], nh=nh, dk=dk, dv=dvh)
    x1, t, ri, rw, counts = _post(xf, og.reshape(b * l, dv), ma.reshape(b * l, d), gb, p["w_out_b"], p["w_o"],
                                  p["g2"], p["w_r"], p["b_r"], n_exp=n_exp, n_grp=n_grp)
    y = _moe(x1, t, ri, rw, counts, p["wg"], p["wu"], p["wd"], p["gf"], n_exp=n_exp)
    wa = prev_a.shape[1]
    wq = prev_qkv.shape[1]
    new_a = r3(u)[:, l - wa:, :].astype(F32)
    new_qkv = r3(qkv)[:, l - wq:, :].astype(F32)
    return y.reshape(b, l, d), new_a, new_qkv, s_fin


def kernel(x_prompt, x_sample, cache_conv_a, cache_conv_qkv, state_gdn, norm1_g, w_in, conv_a_w, w_out_a, conv_qkv_w, a_log, dt_bias, onorm_g, w_out_b, w_o, norm2_g, w_router_group, b_router_group, w_router_expert, b_router_expert, w_gate, w_up, w_down, final_g):
    depth = w_in.shape[0]
    assert depth == 1, "single trunk layer"
    d = x_prompt.shape[-1]
    dc = conv_a_w.shape[-1]
    dq = conv_qkv_w.shape[-1]
    nh = a_log.shape[-1]
    dk, dvh = state_gdn.shape[-2], state_gdn.shape[-1]
    dv = nh * dvh
    n_grp = w_router_group.shape[-1]
    n_exp = w_router_expert.shape[-1]
    dims = (dc, dq, dv, nh, dk, dvh, n_exp, n_grp)

    w = w_in[0]
    o = [0, dc, 2 * dc, 3 * dc, 3 * dc + dq, 3 * dc + dq + dv, 3 * dc + dq + dv + nh, 3 * dc + dq + dv + 2 * nh,
         3 * dc + dq + dv + 2 * nh + d, 3 * dc + dq + dv + 2 * nh + 2 * d]
    seg = lambda k: w[:, o[k]:o[k + 1]]
    w_main = jnp.concatenate([seg(0), seg(2), seg(1), seg(3), seg(4), seg(7), seg(8)], axis=1).astype(BF16)
    w_ab = jnp.pad(jnp.concatenate([seg(5), seg(6)], axis=1), ((0, 0), (0, LANES - 2 * nh))).astype(BF16)
    w_r = jnp.pad(jnp.concatenate([w_router_expert[0], w_router_group[0]], axis=1),
                  ((0, 0), (0, LANES - n_exp - n_grp))).astype(BF16)
    b_r = _lane_row(jnp.concatenate([b_router_expert[0], b_router_group[0]]))
    p = dict(
        g1=norm1_g[0].reshape(1, d), w_main=w_main, w_ab=w_ab,
        conv_a_w=conv_a_w[0], w_out_a=w_out_a[0].astype(BF16),
        conv_qkv_w=conv_qkv_w[0], alog=_lane_row(a_log[0]), dtb=_lane_row(dt_bias[0]),
        onorm_g=onorm_g[0].reshape(1, dvh),
        w_out_b=w_out_b[0].astype(BF16), w_o=w_o[0].astype(BF16), g2=norm2_g[0].reshape(1, d),
        w_r=w_r, b_r=b_r,
        wg=w_gate[0].astype(BF16), wu=w_up[0].astype(BF16), wd=w_down[0].astype(BF16),
        gf=final_g.reshape(1, d),
    )
    bp = x_prompt.shape[0]
    dt_ = x_prompt.dtype
    zero_a = jnp.zeros((bp,) + cache_conv_a.shape[2:], dt_)
    zero_qkv = jnp.zeros((bp,) + cache_conv_qkv.shape[2:], dt_)
    zero_s = jnp.zeros((bp,) + state_gdn.shape[2:], dt_)
    y_p, a_p, q_p, s_p = _run(x_prompt, zero_a, zero_qkv, zero_s, p, dims)
    y_s, a_s, q_s, s_s = _run(x_sample, cache_conv_a[0], cache_conv_qkv[0], state_gdn[0], p, dims)
    return (y_p, y_s, a_p[None], q_p[None], s_p[None], a_s[None], q_s[None], s_s[None])
```

```python
import functools

import jax
import jax.numpy as jnp
from jax import lax
from jax.experimental import pallas as pl
from jax.experimental.pallas import tpu as pltpu

F32 = jnp.float32
BF16 = jnp.bfloat16
EPS = 1e-6
CHUNK = 64
LANES = 128
SUBLANES = 8
VMEM_LIMIT_BYTES = 56 * 1024 * 1024
TOKEN_TILE = 512
EXPERT_TILE = 512
ROW_TILE = 256


def _params(n_axes):
    return pltpu.CompilerParams(dimension_semantics=("arbitrary",) * n_axes,
                                vmem_limit_bytes=VMEM_LIMIT_BYTES)


def _resident(shape):
    return pl.BlockSpec(shape, lambda *_: (0,) * len(shape), pipeline_mode=pl.Buffered(1))


def _dot(a, b):
    return jnp.dot(a, b, preferred_element_type=F32)


def _sigmoid(x):
    return 1.0 / (1.0 + jnp.exp(-x))


def _inproj_body(x_ref, g_ref, w_ref, wab_ref, u_ref, b_ref, qkv_ref, z_ref, ga_ref, gb_ref, ab_ref, *, dc, dq):
    x = x_ref[...]
    r = lax.rsqrt(jnp.mean(x * x, axis=-1, keepdims=True) + EPS)
    xn = (x * r * g_ref[...]).astype(BF16)

    def mm(lo, n):
        return _dot(xn, w_ref[:, lo:lo + n])

    u_ref[...] = (mm(dc, dc) * mm(0, dc)).astype(BF16)
    b_ref[...] = mm(2 * dc, dc).astype(BF16)
    o = 3 * dc
    for j in range(dq // dc):
        qkv_ref[:, j * dc:(j + 1) * dc] = mm(o + j * dc, dc).astype(BF16)
    o += dq
    z_ref[...] = mm(o, z_ref.shape[1]).astype(BF16)
    o += z_ref.shape[1]
    ga_ref[...] = mm(o, ga_ref.shape[1]).astype(BF16)
    o += ga_ref.shape[1]
    gb_ref[...] = mm(o, gb_ref.shape[1]).astype(BF16)
    ab_ref[...] = _dot(xn, wab_ref[...])


def _inproj(x, g1, w_main, w_ab, *, dc, dq, dv, d):
    t = x.shape[0]
    tm = min(TOKEN_TILE, t)
    row = lambda n: pl.BlockSpec((tm, n), lambda i: (i, 0))
    out_shapes = (
        jax.ShapeDtypeStruct((t, dc), BF16), jax.ShapeDtypeStruct((t, dc), BF16),
        jax.ShapeDtypeStruct((t, dq), BF16), jax.ShapeDtypeStruct((t, dv), BF16),
        jax.ShapeDtypeStruct((t, d), BF16), jax.ShapeDtypeStruct((t, d), BF16),
        jax.ShapeDtypeStruct((t, LANES), F32))
    return pl.pallas_call(
        functools.partial(_inproj_body, dc=dc, dq=dq),
        grid=(t // tm,),
        in_specs=[row(d), _resident((1, d)), _resident(w_main.shape), _resident(w_ab.shape)],
        out_specs=(row(dc), row(dc), row(dq), row(dv), row(d), row(d), row(LANES)),
        out_shape=out_shapes,
        compiler_params=_params(1),
        name="inproj",
    )(x, g1, w_main, w_ab)


def _causal_conv(u, prev8, w_ref):
    width = w_ref.shape[0]
    n = u.shape[0]
    body = w_ref[width - 1:width, :] * u
    for sh in range(1, width):
        body = body + w_ref[width - 1 - sh:width - sh, :] * pltpu.roll(u, sh, 0)
    head_in = jnp.concatenate([prev8, u[0:SUBLANES]], axis=0)
    lo = SUBLANES - (width - 1)
    head = w_ref[0:1, :] * head_in[lo:lo + SUBLANES]
    for j in range(1, width):
        head = head + w_ref[j:j + 1, :] * head_in[lo + j:lo + j + SUBLANES]
    if n == SUBLANES:
        return head
    return jnp.concatenate([head, body[SUBLANES:]], axis=0)


def _mixer_a_body(u_ref, b_ref, ga_ref, cache_ref, cw_ref, wo_ref, ma_ref, carry_ref, v_ref, *, bs, tl):
    j = pl.program_id(1)
    for s in range(bs):
        u = u_ref[s].astype(F32)
        prev8 = jnp.where(j == 0, cache_ref[s], carry_ref[s])
        conv = _causal_conv(u, prev8, cw_ref)
        carry_ref[s] = u[tl - SUBLANES:tl]
        v_ref[s * tl:(s + 1) * tl, :] = (b_ref[s].astype(F32) * conv).astype(BF16)
    y = _dot(v_ref[...], wo_ref[...])
    for s in range(bs):
        ga = ga_ref[s].astype(F32)
        ma_ref[s] = (_sigmoid(ga) * y[s * tl:(s + 1) * tl]).astype(BF16)


def _seq_tiles(b, l):
    tl = min(l, TOKEN_TILE)
    bs = max(1, min(b, TOKEN_TILE // tl))
    while b % bs:
        bs -= 1
    return bs, tl


def _mixer_a(u, bg, ga, cache8, conv_w, w_out_a):
    b, l, dc = u.shape
    d = w_out_a.shape[1]
    bs, tl = _seq_tiles(b, l)
    blk = lambda n: pl.BlockSpec((bs, tl, n), lambda i, j: (i, j, 0))
    return pl.pallas_call(
        functools.partial(_mixer_a_body, bs=bs, tl=tl),
        grid=(b // bs, l // tl),
        in_specs=[blk(dc), blk(dc), blk(d),
                  pl.BlockSpec((bs, SUBLANES, dc), lambda i, j: (i, 0, 0)),
                  _resident(conv_w.shape), _resident(w_out_a.shape)],
        out_specs=blk(d),
        out_shape=jax.ShapeDtypeStruct((b, l, d), BF16),
        scratch_shapes=[pltpu.VMEM((bs, SUBLANES, dc), F32), pltpu.VMEM((bs * tl, dc), BF16)],
        compiler_params=_params(2),
        name="mixer_a",
    )(u, bg, ga, cache8, conv_w, w_out_a)


def _unit_lower_inverse(mats):
    n = mats[0].shape[0]
    r = lax.broadcasted_iota(jnp.int32, (n, n), 0)
    c = lax.broadcasted_iota(jnp.int32, (n, n), 1)
    eye = (r == c).astype(F32)

    def same_block(size):
        return jnp.bitwise_xor(r, c) < size

    bb = lambda m: m.astype(BF16)
    a8 = [jnp.where(same_block(SUBLANES), a, 0.0) for a in mats]
    p2 = [bb(_dot(bb(a), bb(a))) for a in a8]
    p4 = [bb(_dot(p, p)) for p in p2]
    xs = [eye - a for a in a8]
    xs = [x + _dot(bb(x), p) for x, p in zip(xs, p2)]
    xs = [x + _dot(bb(x), p) for x, p in zip(xs, p4)]
    size = SUBLANES
    while size < n:
        in_pair = same_block(2 * size) & jnp.logical_not(same_block(size))
        ys = [_dot(bb(jnp.where(in_pair, a, 0.0)), bb(x)) for a, x in zip(mats, xs)]
        xs = [x - _dot(bb(x), bb(y)) for x, y in zip(xs, ys)]
        size *= 2
    return xs


def _gdn_body(qkv_ref, ab_ref, z_ref, cache_ref, s0_ref, cw_ref, alog_ref, dtb_ref, og_ref_g,
              og_ref, sfin_ref, carry_ref, s_ref, *, nb, nh, dk, dv):
    j = pl.program_id(1)
    last = pl.num_programs(1) - 1
    n = CHUNK
    bb = lambda m: m.astype(BF16)

    @pl.when(j == 0)
    def _():
        s_ref[...] = s0_ref[...]

    rows = lax.broadcasted_iota(jnp.int32, (n, LANES), 0)
    ri = lax.broadcasted_iota(jnp.int32, (n, n), 0)
    ci = lax.broadcasted_iota(jnp.int32, (n, n), 1)
    incl = ri >= ci
    strict = ri > ci

    qs, ks, vs, bcols, gcols, gls, decays = [], [], [], [], [], [], []
    for s in range(nb):
        x = qkv_ref[s].astype(F32)
        prev8 = jnp.where(j == 0, cache_ref[s], carry_ref[s])
        xc = _causal_conv(x, prev8, cw_ref)
        carry_ref[s] = x[n - SUBLANES:n]
        xc = xc * _sigmoid(xc)
        ab = ab_ref[s]
        beta = _sigmoid(ab)
        a_in = pltpu.roll(ab, LANES - nh, 1) + dtb_ref[...]
        softplus = jnp.maximum(a_in, 0.0) + jnp.log(1.0 + jnp.exp(-jnp.abs(a_in)))
        gc = -jnp.exp(alog_ref[...]) * softplus
        sh = 1
        while sh < n:
            gc = gc + jnp.where(rows >= sh, pltpu.roll(gc, sh, 0), 0.0)
            sh *= 2
        gct = gc.T
        for h in range(nh):
            q = xc[:, h * dk:(h + 1) * dk]
            k = xc[:, nh * dk + h * dk:nh * dk + (h + 1) * dk]
            qs.append(q * lax.rsqrt(jnp.sum(q * q, axis=-1, keepdims=True) + EPS) * (dk ** -0.5))
            ks.append(k * lax.rsqrt(jnp.sum(k * k, axis=-1, keepdims=True) + EPS))
            vs.append(xc[:, 2 * nh * dk + h * dv:2 * nh * dk + (h + 1) * dv])
            bcols.append(beta[:, h:h + 1])
            gcol = gc[:, h:h + 1]
            gcols.append(gcol)
            gls.append(gc[n - 1:n, h:h + 1])
            decays.append(jnp.where(incl, jnp.exp(jnp.where(incl, gcol - gct[h:h + 1, :], 0.0)), 0.0))

    np_ = nb * nh
    qk_kk = [lax.dot_general(bb(jnp.concatenate([qs[p], ks[p]], axis=0)), bb(ks[p]),
                             (((1,), (1,)), ((), ())), preferred_element_type=F32) for p in range(np_)]
    attn = [bb(qk_kk[p][:n] * decays[p]) for p in range(np_)]
    ts = _unit_lower_inverse([jnp.where(strict, bcols[p] * qk_kk[p][n:] * decays[p], 0.0) for p in range(np_)])
    egs = [jnp.exp(g) for g in gcols]
    uw = [_dot(bb(ts[p]), bb(jnp.concatenate([bcols[p] * vs[p], (bcols[p] * egs[p]) * ks[p]], axis=1)))
          for p in range(np_)]
    kdt = [bb((ks[p] * jnp.exp(gls[p] - gcols[p])).T) for p in range(np_)]

    sidx = [(p // nh, p % nh) for p in range(np_)]
    s_old = [s_ref[a, h] for a, h in sidx]
    ws = [_dot(bb(jnp.concatenate([uw[p][:, dv:], qs[p] * egs[p]], axis=0)), bb(s_old[p])) for p in range(np_)]
    v_new = [bb(uw[p][:, :dv] - ws[p][:n]) for p in range(np_)]
    o_in = [_dot(attn[p], v_new[p]) for p in range(np_)]
    ds = [_dot(kdt[p], v_new[p]) for p in range(np_)]
    for p, (a, h) in enumerate(sidx):
        s_ref[a, h] = s_old[p] * jnp.exp(gls[p]) + ds[p]
        o = ws[p][n:] + o_in[p]
        on = o * lax.rsqrt(jnp.mean(o * o, axis=-1, keepdims=True) + EPS) * og_ref_g[...]
        zz = z_ref[a, :, h * dv:(h + 1) * dv].astype(F32)
        og_ref[a, :, h * dv:(h + 1) * dv] = (on * (zz * _sigmoid(zz))).astype(BF16)

    @pl.when(j == last)
    def _():
        sfin_ref[...] = s_ref[...]


GDN_SEQS_PER_STEP = 4


def _gdn(qkv, ab, z, cache8, s0, conv_w, alog, dtb, onorm_g, *, nh, dk, dv):
    b, l, dq = qkv.shape
    n = CHUNK
    nb = GDN_SEQS_PER_STEP if b % GDN_SEQS_PER_STEP == 0 else 1
    blk = lambda c: pl.BlockSpec((nb, n, c), lambda i, j: (i, j, 0))
    state = pl.BlockSpec((nb, nh, dk, dv), lambda i, j: (i, 0, 0, 0))
    return pl.pallas_call(
        functools.partial(_gdn_body, nb=nb, nh=nh, dk=dk, dv=dv),
        grid=(b // nb, l // n),
        in_specs=[blk(dq), blk(LANES), blk(nh * dv),
                  pl.BlockSpec((nb, SUBLANES, dq), lambda i, j: (i, 0, 0)), state,
                  _resident(conv_w.shape), _resident(alog.shape), _resident(dtb.shape), _resident(onorm_g.shape)],
        out_specs=(blk(nh * dv), state),
        out_shape=(jax.ShapeDtypeStruct((b, l, nh * dv), BF16), jax.ShapeDtypeStruct((b, nh, dk, dv), F32)),
        scratch_shapes=[pltpu.VMEM((nb, SUBLANES, dq), F32), pltpu.VMEM((nb, nh, dk, dv), F32)],
        compiler_params=_params(2),
        name="gdn",
    )(qkv, ab, z, cache8, s0, conv_w, alog, dtb, onorm_g)


def _post_body(x_ref, og_ref, ma_ref, gb_ref, wob_ref, wo_ref, g2_ref, wr_ref, br_ref,
               x1_ref, t_ref, ri_ref, rw_ref, cnt_ref, base_ref, *, n_exp, n_grp, epg):
    i = pl.program_id(0)
    tm = x_ref.shape[0]

    @pl.when(i == 0)
    def _():
        base_ref[...] = jnp.zeros_like(base_ref)

    y_b = _dot(og_ref[...], wob_ref[...])
    m = ma_ref[...].astype(F32) + _sigmoid(gb_ref[...].astype(F32)) * y_b
    x1 = x_ref[...] + _dot(m.astype(BF16), wo_ref[...])
    x1_ref[...] = x1
    r = lax.rsqrt(jnp.mean(x1 * x1, axis=-1, keepdims=True) + EPS)
    t = x1 * r * g2_ref[...]
    t_ref[...] = t
    logits = _dot(t.astype(BF16), wr_ref[...]) + br_ref[...]
    lane = lax.broadcasted_iota(jnp.int32, (tm, LANES), 1)
    lane_f = lane.astype(F32)
    neg = -jnp.inf

    def first_max(v):
        mx = jnp.max(v, axis=-1, keepdims=True)
        idx = jnp.min(jnp.where(v == mx, lane_f, float(LANES)), axis=-1, keepdims=True)
        return mx, idx.astype(jnp.int32)

    gmask = (lane >= n_exp) & (lane < n_exp + n_grp)
    gmax, gidx = first_max(jnp.where(gmask, logits, neg))
    gsel = gidx - n_exp
    pg_sel = 1.0 / jnp.sum(jnp.where(gmask, jnp.exp(logits - gmax), 0.0), axis=-1, keepdims=True)
    emask = (lane >= gsel * epg) & (lane < (gsel + 1) * epg)
    le = jnp.where(emask, logits, neg)
    m1, e1 = first_max(le)
    m2, e2 = first_max(jnp.where(lane == e1, neg, le))
    ex = jnp.exp(m2 - m1)
    p1 = 1.0 / (1.0 + ex)
    p2 = ex * p1
    onehot = ((lane == e1) | (lane == e2)).astype(BF16)
    rr = lax.broadcasted_iota(jnp.int32, (tm, tm), 0)
    cc = lax.broadcasted_iota(jnp.int32, (tm, tm), 1)
    before = _dot((rr > cc).astype(BF16), onehot) + base_ref[0:1, :]
    rank1 = jnp.sum(jnp.where(lane == e1, before, 0.0), axis=-1, keepdims=True).astype(jnp.int32)
    rank2 = jnp.sum(jnp.where(lane == e2, before, 0.0), axis=-1, keepdims=True).astype(jnp.int32)
    base_ref[0:1, :] = base_ref[0:1, :] + jnp.sum(onehot.astype(F32), axis=0, keepdims=True)
    ri_ref[...] = jnp.where(lane == 0, e1, jnp.where(lane == 1, e2, jnp.where(lane == 2, rank1, jnp.where(lane == 3, rank2, 0))))
    rw_ref[...] = jnp.where(lane == 0, pg_sel * p1, jnp.where(lane == 1, pg_sel * p2, 0.0))
    cnt_ref[...] = base_ref[...]


def _post(x, og, ma, gb, w_out_b, w_o, g2, w_r, b_r, *, n_exp, n_grp):
    t, d = x.shape
    tm = min(TOKEN_TILE, t)
    row = lambda n: pl.BlockSpec((tm, n), lambda i: (i, 0))
    return pl.pallas_call(
        functools.partial(_post_body, n_exp=n_exp, n_grp=n_grp, epg=n_exp // n_grp),
        grid=(t // tm,),
        in_specs=[row(d), row(og.shape[1]), row(d), row(d), _resident(w_out_b.shape), _resident(w_o.shape),
                  _resident(g2.shape), _resident(w_r.shape), _resident(b_r.shape)],
        out_specs=(row(d), row(d), row(LANES), row(LANES), pl.BlockSpec((SUBLANES, LANES), lambda i: (0, 0))),
        out_shape=(jax.ShapeDtypeStruct((t, d), F32), jax.ShapeDtypeStruct((t, d), F32),
                   jax.ShapeDtypeStruct((t, LANES), jnp.int32), jax.ShapeDtypeStruct((t, LANES), F32),
                   jax.ShapeDtypeStruct((SUBLANES, LANES), F32)),
        scratch_shapes=[pltpu.VMEM((SUBLANES, LANES), F32)],
        compiler_params=_params(1),
        name="post",
    )(x, og, ma, gb, w_out_b, w_o, g2, w_r, b_r)


ROW_UNROLL = 8


def _scatter_body(ztile_ref, dest_ref, t_ref, xs_ref, zero_ref, sem, zsem, *, ts, r, n_zero):
    i = pl.program_id(0)
    last = pl.num_programs(0) - 1

    def zero_copy(e):
        return pltpu.make_async_copy(zero_ref, xs_ref.at[pl.ds(pl.multiple_of(ztile_ref[e], r), r)], zsem)

    @pl.when(i == 0)
    def _():
        zero_ref[...] = jnp.zeros_like(zero_ref)
        for e in range(n_zero):
            @pl.when(ztile_ref[e] >= 0)
            def _():
                zero_copy(e).start()
        for e in range(n_zero):
            @pl.when(ztile_ref[e] >= 0)
            def _():
                zero_copy(e).wait()

    def start(q, c):
        row = i * ts + q
        for k in range(2):
            pltpu.make_async_copy(t_ref.at[pl.ds(row, 1)], xs_ref.at[pl.ds(dest_ref[0, 0, 2 * q + k], 1)], sem).start()
        return c

    def wait_step():
        for _ in range(2):
            pltpu.make_async_copy(t_ref.at[pl.ds(0, ts)], xs_ref.at[pl.ds(0, ts)], sem).wait()

    lax.fori_loop(0, ts, start, 0, unroll=ROW_UNROLL)

    @pl.when(i > 0)
    def _():
        wait_step()

    @pl.when(i == last)
    def _():
        wait_step()


def _scatter(zero_tile_start, dest, t, n_rows, r):
    n_tok, d = t.shape
    ts = min(ROW_TILE, n_tok)
    dest3 = dest.reshape(n_tok // ts, 1, 2 * ts)
    grid_spec = pltpu.PrefetchScalarGridSpec(
        num_scalar_prefetch=1,
        grid=(n_tok // ts,),
        in_specs=[pl.BlockSpec((1, 1, 2 * ts), lambda i, zt: (i, 0, 0), memory_space=pltpu.SMEM),
                  pl.BlockSpec(memory_space=pl.ANY)],
        out_specs=pl.BlockSpec(memory_space=pl.ANY),
        scratch_shapes=[pltpu.VMEM((r, d), F32), pltpu.SemaphoreType.DMA(()), pltpu.SemaphoreType.DMA(())],
    )
    return pl.pallas_call(
        functools.partial(_scatter_body, ts=ts, r=r, n_zero=zero_tile_start.shape[0]),
        grid_spec=grid_spec,
        out_shape=jax.ShapeDtypeStruct((n_rows, d), F32),
        compiler_params=_params(1),
        name="moe_scatter",
    )(zero_tile_start, dest3, t)


def _experts_body(te_ref, nt_ref, xs_ref, wg_ref, wu_ref, wd_ref, ys_ref):
    i = pl.program_id(0)

    @pl.when(i < nt_ref[0])
    def _():
        x = xs_ref[...].astype(BF16)
        hg = _dot(x, wg_ref[...])
        hu = _dot(x, wu_ref[...])
        hdn = (hg * _sigmoid(hg)) * hu
        ys_ref[...] = _dot(hdn.astype(BF16), wd_ref[...])

    @pl.when(i >= nt_ref[0])
    def _():
        ys_ref[...] = jnp.zeros_like(ys_ref)


def _experts(tile_expert, n_tiles_used, xs, wg, wu, wd, r):
    rows, d = xs.shape
    de = wg.shape[2]
    used = lambda i, nt: jnp.minimum(i, nt[0] - 1)
    grid_spec = pltpu.PrefetchScalarGridSpec(
        num_scalar_prefetch=2,
        grid=(rows // r,),
        in_specs=[pl.BlockSpec((r, d), lambda i, te, nt: (used(i, nt), 0)),
                  pl.BlockSpec((None, d, de), lambda i, te, nt: (te[i], 0, 0)),
                  pl.BlockSpec((None, d, de), lambda i, te, nt: (te[i], 0, 0)),
                  pl.BlockSpec((None, de, d), lambda i, te, nt: (te[i], 0, 0))],
        out_specs=pl.BlockSpec((r, d), lambda i, te, nt: (i, 0)),
    )
    return pl.pallas_call(
        _experts_body,
        grid_spec=grid_spec,
        out_shape=jax.ShapeDtypeStruct((rows, d), F32),
        compiler_params=_params(1),
        name="moe_experts",
    )(tile_expert, n_tiles_used, xs, wg, wu, wd)


def _combine_body(dnext_ref, dfirst_ref, x1_ref, rw_ref, ys_ref, gf_ref, y_ref, buf_ref, sem, *, ts):
    i = pl.program_id(0)
    n = pl.num_programs(0)

    def gather(dref, slot):
        def start(q, c):
            for k in range(2):
                pltpu.make_async_copy(ys_ref.at[pl.ds(dref[0, 0, 2 * q + k], 1)],
                                      buf_ref.at[slot, k, pl.ds(q, 1)], sem.at[slot]).start()
            return c
        lax.fori_loop(0, ts, start, 0, unroll=ROW_UNROLL)

    @pl.when(i == 0)
    def _():
        gather(dfirst_ref, 0)

    @pl.when(i + 1 < n)
    def _():
        gather(dnext_ref, (i + 1) % 2)

    slot = i % 2
    for k in range(2):
        pltpu.make_async_copy(ys_ref.at[pl.ds(0, ts)], buf_ref.at[slot, k], sem.at[slot]).wait()
    rw = rw_ref[...]
    acc = rw[:, 0:1] * buf_ref[slot, 0] + rw[:, 1:2] * buf_ref[slot, 1]
    x2 = x1_ref[...] + acc
    r = lax.rsqrt(jnp.mean(x2 * x2, axis=-1, keepdims=True) + EPS)
    y_ref[...] = x2 * r * gf_ref[...]


def _combine(dest, x1, rw, ys, gf):
    n_tok, d = x1.shape
    ts = min(ROW_TILE, n_tok)
    n = n_tok // ts
    dest3 = dest.reshape(n, 1, 2 * ts)
    return pl.pallas_call(
        functools.partial(_combine_body, ts=ts),
        grid=(n,),
        in_specs=[pl.BlockSpec((1, 1, 2 * ts), lambda i: (jnp.minimum(i + 1, n - 1), 0, 0), memory_space=pltpu.SMEM),
                  pl.BlockSpec((1, 1, 2 * ts), lambda i: (0, 0, 0), memory_space=pltpu.SMEM),
                  pl.BlockSpec((ts, d), lambda i: (i, 0)),
                  pl.BlockSpec((ts, LANES), lambda i: (i, 0)),
                  pl.BlockSpec(memory_space=pl.ANY),
                  _resident(gf.shape)],
        out_specs=pl.BlockSpec((ts, d), lambda i: (i, 0)),
        out_shape=jax.ShapeDtypeStruct((n_tok, d), F32),
        scratch_shapes=[pltpu.VMEM((2, 2, ts, d), F32), pltpu.SemaphoreType.DMA((2,))],
        compiler_params=_params(1),
        name="moe_combine",
    )(dest3, dest3, x1, rw, ys, gf)


def _expert_tile(n_tok, n_exp):
    r = EXPERT_TILE
    while r > LANES and r > (2 * n_tok) // n_exp:
        r //= 2
    return r


def _moe(x1, t, ri, rw, counts, wg, wu, wd, gf, *, n_exp):
    n_tok, d = x1.shape
    r = _expert_tile(n_tok, n_exp)
    cnt = counts[0, :n_exp].astype(jnp.int32)
    padded = ((cnt + r - 1) // r) * r
    ends = jnp.cumsum(padded)
    offs = ends - padded
    n_tiles = (2 * n_tok) // r + n_exp
    tile_start = jnp.arange(n_tiles, dtype=jnp.int32) * r
    tile_expert = jnp.minimum(jnp.sum(tile_start[:, None] >= ends[None, :], axis=1), n_exp - 1).astype(jnp.int32)
    n_used = (ends[-1] // r).astype(jnp.int32).reshape(1)
    tail = ends[-1] + jnp.arange(n_exp, dtype=jnp.int32) * r
    zero_tile_start = jnp.concatenate([jnp.where(cnt > 0, ends - r, -1),
                                       jnp.where(tail < n_tiles * r, tail, -1)]).astype(jnp.int32)
    dest = jnp.take(offs, ri[:, 0:2], axis=0) + ri[:, 2:4]
    xs = _scatter(zero_tile_start, dest, t, n_tiles * r, r)
    ys = _experts(tile_expert, n_used, xs, wg, wu, wd, r)
    return _combine(dest, x1, rw, ys, gf)


def _pad_rows8(c):
    return jnp.pad(c, ((0, 0), (SUBLANES - c.shape[1], 0), (0, 0)))


def _lane_row(v):
    return jnp.pad(v.astype(F32), (0, LANES - v.shape[0])).reshape(1, LANES)


def _run(x, prev_a, prev_qkv, s0, p, dims):
    b, l, d = x.shape
    dc, dq, dv, nh, dk, dvh, n_exp, n_grp = dims
    xf = x.reshape(b * l, d)
    u, bg, qkv, z, ga, gb, ab = _inproj(xf, p["g1"], p["w_main"], p["w_ab"], dc=dc, dq=dq, dv=dv, d=d)
    r3 = lambda a: a.reshape(b, l, a.shape[-1])
    ma = _mixer_a(r3(u), r3(bg), r3(ga), _pad_rows8(prev_a), p["conv_a_w"], p["w_out_a"])
    og, s_fin = _gdn(r3(qkv), r3(ab), r3(z), _pad_rows8(prev_qkv), s0, p["conv_qkv_w"], p["alog"], p["dtb"],
                     p["onorm_g"], nh=nh, dk=dk, dv=dvh)
    x1, t, ri, rw, counts = _post(xf, og.reshape(b * l, dv), ma.reshape(b * l, d), gb, p["w_out_b"], p["w_o"],
                                  p["g2"], p["w_r"], p["b_r"], n_exp=n_exp, n_grp=n_grp)
    y = _moe(x1, t, ri, rw, counts, p["wg"], p["wu"], p["wd"], p["gf"], n_exp=n_exp)
    wa = prev_a.shape[1]
    wq = prev_qkv.shape[1]
    new_a = r3(u)[:, l - wa:, :].astype(F32)
    new_qkv = r3(qkv)[:, l - wq:, :].astype(F32)
    return y.reshape(b, l, d), new_a, new_qkv, s_fin


def kernel(x_prompt, x_sample, cache_conv_a, cache_conv_qkv, state_gdn, norm1_g, w_in, conv_a_w, w_out_a, conv_qkv_w, a_log, dt_bias, onorm_g, w_out_b, w_o, norm2_g, w_router_group, b_router_group, w_router_expert, b_router_expert, w_gate, w_up, w_down, final_g):
    depth = w_in.shape[0]
    assert depth == 1, "single trunk layer"
    d = x_prompt.shape[-1]
    dc = conv_a_w.shape[-1]
    dq = conv_qkv_w.shape[-1]
    nh = a_log.shape[-1]
    dk, dvh = state_gdn.shape[-2], state_gdn.shape[-1]
    dv = nh * dvh
    n_grp = w_router_group.shape[-1]
    n_exp = w_router_expert.shape[-1]
    dims = (dc, dq, dv, nh, dk, dvh, n_exp, n_grp)

    w = w_in[0]
    o = [0, dc, 2 * dc, 3 * dc, 3 * dc + dq, 3 * dc + dq + dv, 3 * dc + dq + dv + nh, 3 * dc + dq + dv + 2 * nh,
         3 * dc + dq + dv + 2 * nh + d, 3 * dc + dq + dv + 2 * nh + 2 * d]
    seg = lambda k: w[:, o[k]:o[k + 1]]
    w_main = jnp.concatenate([seg(0), seg(2), seg(1), seg(3), seg(4), seg(7), seg(8)], axis=1).astype(BF16)
    w_ab = jnp.pad(jnp.concatenate([seg(5), seg(6)], axis=1), ((0, 0), (0, LANES - 2 * nh))).astype(BF16)
    w_r = jnp.pad(jnp.concatenate([w_router_expert[0], w_router_group[0]], axis=1),
                  ((0, 0), (0, LANES - n_exp - n_grp))).astype(BF16)
    b_r = _lane_row(jnp.concatenate([b_router_expert[0], b_router_group[0]]))
    p = dict(
        g1=norm1_g[0].reshape(1, d), w_main=w_main, w_ab=w_ab,
        conv_a_w=conv_a_w[0], w_out_a=w_out_a[0].astype(BF16),
        conv_qkv_w=conv_qkv_w[0], alog=_lane_row(a_log[0]), dtb=_lane_row(dt_bias[0]),
        onorm_g=onorm_g[0].reshape(1, dvh),
        w_out_b=w_out_b[0].astype(BF16), w_o=w_o[0].astype(BF16), g2=norm2_g[0].reshape(1, d),
        w_r=w_r, b_r=b_r,
        wg=w_gate[0].astype(BF16), wu=w_up[0].astype(BF16), wd=w_down[0].astype(BF16),
        gf=final_g.reshape(1, d),
    )
    bp = x_prompt.shape[0]
    dt_ = x_prompt.dtype
    zero_a = jnp.zeros((bp,) + cache_conv_a.shape[2:], dt_)
    zero_qkv = jnp.zeros((bp,) + cache_conv_qkv.shape[2:], dt_)
    zero_s = jnp.zeros((bp,) + state_gdn.shape[2:], dt_)
    y_p, a_p, q_p, s_p = _run(x_prompt, zero_a, zero_qkv, zero_s, p, dims)
    y_s, a_s, q_s, s_s = _run(x_sample, cache_conv_a[0], cache_conv_qkv[0], state_gdn[0], p, dims)
    return (y_p, y_s, a_p[None], q_p[None], s_p[None], a_s[None], q_s[None], s_s[None])
```

```python
import functools

import jax
import jax.numpy as jnp
from jax import lax
from jax.experimental import pallas as pl
from jax.experimental.pallas import tpu as pltpu

F32 = jnp.float32
BF16 = jnp.bfloat16
EPS = 1e-6
CHUNK = 64
LANES = 128
SUBLANES = 8
VMEM_LIMIT_BYTES = 56 * 1024 * 1024
TOKEN_TILE = 512
EXPERT_TILE = 512
ROW_TILE = 256


def _params(n_axes):
    return pltpu.CompilerParams(dimension_semantics=("arbitrary",) * n_axes,
                                vmem_limit_bytes=VMEM_LIMIT_BYTES)


def _resident(shape):
    return pl.BlockSpec(shape, lambda *_: (0,) * len(shape), pipeline_mode=pl.Buffered(1))


def _dot(a, b):
    return jnp.dot(a, b, preferred_element_type=F32)


def _sigmoid(x):
    return 1.0 / (1.0 + jnp.exp(-x))


def _store_row_tiles(ref, x):
    n = x.shape[0]
    for c in range(SUBLANES):
        ref[pl.ds(c, n, stride=SUBLANES), :] = x[:, c * LANES:(c + 1) * LANES]


def _load_row_tiles(ref, n):
    return jnp.concatenate([ref[pl.ds(c, n, stride=SUBLANES), :] for c in range(SUBLANES)], axis=1)


def _inproj_body(x_ref, g_ref, w_ref, wab_ref, u_ref, b_ref, qkv_ref, z_ref, ga_ref, gb_ref, ab_ref, *, dc, dq):
    x = x_ref[...]
    r = lax.rsqrt(jnp.mean(x * x, axis=-1, keepdims=True) + EPS)
    xn = (x * r * g_ref[...]).astype(BF16)

    def mm(lo, n):
        return _dot(xn, w_ref[:, lo:lo + n])

    u_ref[...] = (mm(dc, dc) * mm(0, dc)).astype(BF16)
    b_ref[...] = mm(2 * dc, dc).astype(BF16)
    o = 3 * dc
    for j in range(dq // dc):
        qkv_ref[:, j * dc:(j + 1) * dc] = mm(o + j * dc, dc).astype(BF16)
    o += dq
    z_ref[...] = mm(o, z_ref.shape[1]).astype(BF16)
    o += z_ref.shape[1]
    ga_ref[...] = mm(o, ga_ref.shape[1]).astype(BF16)
    o += ga_ref.shape[1]
    gb_ref[...] = mm(o, gb_ref.shape[1]).astype(BF16)
    ab_ref[...] = _dot(xn, wab_ref[...])


def _inproj(x, g1, w_main, w_ab, *, dc, dq, dv, d):
    t = x.shape[0]
    tm = min(TOKEN_TILE, t)
    row = lambda n: pl.BlockSpec((tm, n), lambda i: (i, 0))
    out_shapes = (
        jax.ShapeDtypeStruct((t, dc), BF16), jax.ShapeDtypeStruct((t, dc), BF16),
        jax.ShapeDtypeStruct((t, dq), BF16), jax.ShapeDtypeStruct((t, dv), BF16),
        jax.ShapeDtypeStruct((t, d), BF16), jax.ShapeDtypeStruct((t, d), BF16),
        jax.ShapeDtypeStruct((t, LANES), F32))
    return pl.pallas_call(
        functools.partial(_inproj_body, dc=dc, dq=dq),
        grid=(t // tm,),
        in_specs=[row(d), _resident((1, d)), _resident(w_main.shape), _resident(w_ab.shape)],
        out_specs=(row(dc), row(dc), row(dq), row(dv), row(d), row(d), row(LANES)),
        out_shape=out_shapes,
        compiler_params=_params(1),
        name="inproj",
    )(x, g1, w_main, w_ab)


def _causal_conv(u, prev8, w_ref):
    width = w_ref.shape[0]
    n = u.shape[0]
    body = w_ref[width - 1:width, :] * u
    for sh in range(1, width):
        body = body + w_ref[width - 1 - sh:width - sh, :] * pltpu.roll(u, sh, 0)
    head_in = jnp.concatenate([prev8, u[0:SUBLANES]], axis=0)
    lo = SUBLANES - (width - 1)
    head = w_ref[0:1, :] * head_in[lo:lo + SUBLANES]
    for j in range(1, width):
        head = head + w_ref[j:j + 1, :] * head_in[lo + j:lo + j + SUBLANES]
    if n == SUBLANES:
        return head
    return jnp.concatenate([head, body[SUBLANES:]], axis=0)


def _mixer_a_body(u_ref, b_ref, ga_ref, cache_ref, cw_ref, wo_ref, ma_ref, carry_ref, v_ref, *, bs, tl):
    j = pl.program_id(1)
    for s in range(bs):
        u = u_ref[s].astype(F32)
        prev8 = jnp.where(j == 0, cache_ref[s], carry_ref[s])
        conv = _causal_conv(u, prev8, cw_ref)
        carry_ref[s] = u[tl - SUBLANES:tl]
        v_ref[s * tl:(s + 1) * tl, :] = (b_ref[s].astype(F32) * conv).astype(BF16)
    y = _dot(v_ref[...], wo_ref[...])
    for s in range(bs):
        ga = ga_ref[s].astype(F32)
        ma_ref[s] = (_sigmoid(ga) * y[s * tl:(s + 1) * tl]).astype(BF16)


def _seq_tiles(b, l):
    tl = min(l, TOKEN_TILE)
    bs = max(1, min(b, TOKEN_TILE // tl))
    while b % bs:
        bs -= 1
    return bs, tl


def _mixer_a(u, bg, ga, cache8, conv_w, w_out_a):
    b, l, dc = u.shape
    d = w_out_a.shape[1]
    bs, tl = _seq_tiles(b, l)
    blk = lambda n: pl.BlockSpec((bs, tl, n), lambda i, j: (i, j, 0))
    return pl.pallas_call(
        functools.partial(_mixer_a_body, bs=bs, tl=tl),
        grid=(b // bs, l // tl),
        in_specs=[blk(dc), blk(dc), blk(d),
                  pl.BlockSpec((bs, SUBLANES, dc), lambda i, j: (i, 0, 0)),
                  _resident(conv_w.shape), _resident(w_out_a.shape)],
        out_specs=blk(d),
        out_shape=jax.ShapeDtypeStruct((b, l, d), BF16),
        scratch_shapes=[pltpu.VMEM((bs, SUBLANES, dc), F32), pltpu.VMEM((bs * tl, dc), BF16)],
        compiler_params=_params(2),
        name="mixer_a",
    )(u, bg, ga, cache8, conv_w, w_out_a)


def _unit_lower_inverse(mats):
    n = mats[0].shape[0]
    r = lax.broadcasted_iota(jnp.int32, (n, n), 0)
    c = lax.broadcasted_iota(jnp.int32, (n, n), 1)
    eye = (r == c).astype(F32)

    def same_block(size):
        return jnp.bitwise_xor(r, c) < size

    bb = lambda m: m.astype(BF16)
    a8 = [jnp.where(same_block(SUBLANES), a, 0.0) for a in mats]
    p2 = [bb(_dot(bb(a), bb(a))) for a in a8]
    p4 = [bb(_dot(p, p)) for p in p2]
    xs = [eye - a for a in a8]
    xs = [x + _dot(bb(x), p) for x, p in zip(xs, p2)]
    xs = [x + _dot(bb(x), p) for x, p in zip(xs, p4)]
    size = SUBLANES
    while size < n:
        in_pair = same_block(2 * size) & jnp.logical_not(same_block(size))
        ys = [_dot(bb(jnp.where(in_pair, a, 0.0)), bb(x)) for a, x in zip(mats, xs)]
        xs = [x - _dot(bb(x), bb(y)) for x, y in zip(xs, ys)]
        size *= 2
    return xs


def _gdn_body(qkv_ref, ab_ref, z_ref, cache_ref, s0_ref, cw_ref, alog_ref, dtb_ref, og_ref_g,
              og_ref, sfin_ref, carry_ref, s_ref, *, nb, nh, dk, dv):
    j = pl.program_id(1)
    last = pl.num_programs(1) - 1
    n = CHUNK
    bb = lambda m: m.astype(BF16)

    @pl.when(j == 0)
    def _():
        s_ref[...] = s0_ref[...]

    rows = lax.broadcasted_iota(jnp.int32, (n, LANES), 0)
    ri = lax.broadcasted_iota(jnp.int32, (n, n), 0)
    ci = lax.broadcasted_iota(jnp.int32, (n, n), 1)
    incl = ri >= ci
    strict = ri > ci

    qs, ks, vs, bcols, gcols, gls, decays = [], [], [], [], [], [], []
    for s in range(nb):
        x = qkv_ref[s].astype(F32)
        prev8 = jnp.where(j == 0, cache_ref[s], carry_ref[s])
        xc = _causal_conv(x, prev8, cw_ref)
        carry_ref[s] = x[n - SUBLANES:n]
        xc = xc * _sigmoid(xc)
        ab = ab_ref[s]
        beta = _sigmoid(ab)
        a_in = pltpu.roll(ab, LANES - nh, 1) + dtb_ref[...]
        softplus = jnp.maximum(a_in, 0.0) + jnp.log(1.0 + jnp.exp(-jnp.abs(a_in)))
        gc = -jnp.exp(alog_ref[...]) * softplus
        sh = 1
        while sh < n:
            gc = gc + jnp.where(rows >= sh, pltpu.roll(gc, sh, 0), 0.0)
            sh *= 2
        gct = gc.T
        for h in range(nh):
            q = xc[:, h * dk:(h + 1) * dk]
            k = xc[:, nh * dk + h * dk:nh * dk + (h + 1) * dk]
            qs.append(q * lax.rsqrt(jnp.sum(q * q, axis=-1, keepdims=True) + EPS) * (dk ** -0.5))
            ks.append(k * lax.rsqrt(jnp.sum(k * k, axis=-1, keepdims=True) + EPS))
            vs.append(xc[:, 2 * nh * dk + h * dv:2 * nh * dk + (h + 1) * dv])
            bcols.append(beta[:, h:h + 1])
            gcol = gc[:, h:h + 1]
            gcols.append(gcol)
            gls.append(gc[n - 1:n, h:h + 1])
            decays.append(jnp.where(incl, jnp.exp(jnp.where(incl, gcol - gct[h:h + 1, :], 0.0)), 0.0))

    np_ = nb * nh
    qk_kk = [lax.dot_general(bb(jnp.concatenate([qs[p], ks[p]], axis=0)), bb(ks[p]),
                             (((1,), (1,)), ((), ())), preferred_element_type=F32) for p in range(np_)]
    attn = [bb(qk_kk[p][:n] * decays[p]) for p in range(np_)]
    ts = _unit_lower_inverse([jnp.where(strict, bcols[p] * qk_kk[p][n:] * decays[p], 0.0) for p in range(np_)])
    egs = [jnp.exp(g) for g in gcols]
    uw = [_dot(bb(ts[p]), bb(jnp.concatenate([bcols[p] * vs[p], (bcols[p] * egs[p]) * ks[p]], axis=1)))
          for p in range(np_)]
    kdt = [bb((ks[p] * jnp.exp(gls[p] - gcols[p])).T) for p in range(np_)]

    sidx = [(p // nh, p % nh) for p in range(np_)]
    s_old = [s_ref[a, h] for a, h in sidx]
    ws = [_dot(bb(jnp.concatenate([uw[p][:, dv:], qs[p] * egs[p]], axis=0)), bb(s_old[p])) for p in range(np_)]
    v_new = [bb(uw[p][:, :dv] - ws[p][:n]) for p in range(np_)]
    o_in = [_dot(attn[p], v_new[p]) for p in range(np_)]
    ds = [_dot(kdt[p], v_new[p]) for p in range(np_)]
    for p, (a, h) in enumerate(sidx):
        s_ref[a, h] = s_old[p] * jnp.exp(gls[p]) + ds[p]
        o = ws[p][n:] + o_in[p]
        on = o * lax.rsqrt(jnp.mean(o * o, axis=-1, keepdims=True) + EPS) * og_ref_g[...]
        zz = z_ref[a, :, h * dv:(h + 1) * dv].astype(F32)
        og_ref[a, :, h * dv:(h + 1) * dv] = (on * (zz * _sigmoid(zz))).astype(BF16)

    @pl.when(j == last)
    def _():
        sfin_ref[...] = s_ref[...]


GDN_SEQS_PER_STEP = 4


def _gdn(qkv, ab, z, cache8, s0, conv_w, alog, dtb, onorm_g, *, nh, dk, dv):
    b, l, dq = qkv.shape
    n = CHUNK
    nb = GDN_SEQS_PER_STEP if b % GDN_SEQS_PER_STEP == 0 else 1
    blk = lambda c: pl.BlockSpec((nb, n, c), lambda i, j: (i, j, 0))
    state = pl.BlockSpec((nb, nh, dk, dv), lambda i, j: (i, 0, 0, 0))
    return pl.pallas_call(
        functools.partial(_gdn_body, nb=nb, nh=nh, dk=dk, dv=dv),
        grid=(b // nb, l // n),
        in_specs=[blk(dq), blk(LANES), blk(nh * dv),
                  pl.BlockSpec((nb, SUBLANES, dq), lambda i, j: (i, 0, 0)), state,
                  _resident(conv_w.shape), _resident(alog.shape), _resident(dtb.shape), _resident(onorm_g.shape)],
        out_specs=(blk(nh * dv), state),
        out_shape=(jax.ShapeDtypeStruct((b, l, nh * dv), BF16), jax.ShapeDtypeStruct((b, nh, dk, dv), F32)),
        scratch_shapes=[pltpu.VMEM((nb, SUBLANES, dq), F32), pltpu.VMEM((nb, nh, dk, dv), F32)],
        compiler_params=_params(2),
        name="gdn",
    )(qkv, ab, z, cache8, s0, conv_w, alog, dtb, onorm_g)


def _post_body(x_ref, og_ref, ma_ref, gb_ref, wob_ref, wo_ref, g2_ref, wr_ref, br_ref,
               x1_ref, t_ref, ri_ref, rw_ref, cnt_ref, base_ref, *, n_exp, n_grp, epg):
    i = pl.program_id(0)
    tm = x_ref.shape[0]

    @pl.when(i == 0)
    def _():
        base_ref[...] = jnp.zeros_like(base_ref)

    y_b = _dot(og_ref[...], wob_ref[...])
    m = ma_ref[...].astype(F32) + _sigmoid(gb_ref[...].astype(F32)) * y_b
    x1 = x_ref[...] + _dot(m.astype(BF16), wo_ref[...])
    x1_ref[...] = x1
    r = lax.rsqrt(jnp.mean(x1 * x1, axis=-1, keepdims=True) + EPS)
    t = x1 * r * g2_ref[...]
    _store_row_tiles(t_ref, t)
    logits = _dot(t.astype(BF16), wr_ref[...]) + br_ref[...]
    lane = lax.broadcasted_iota(jnp.int32, (tm, LANES), 1)
    lane_f = lane.astype(F32)
    neg = -jnp.inf

    def first_max(v):
        mx = jnp.max(v, axis=-1, keepdims=True)
        idx = jnp.min(jnp.where(v == mx, lane_f, float(LANES)), axis=-1, keepdims=True)
        return mx, idx.astype(jnp.int32)

    gmask = (lane >= n_exp) & (lane < n_exp + n_grp)
    gmax, gidx = first_max(jnp.where(gmask, logits, neg))
    gsel = gidx - n_exp
    pg_sel = 1.0 / jnp.sum(jnp.where(gmask, jnp.exp(logits - gmax), 0.0), axis=-1, keepdims=True)
    emask = (lane >= gsel * epg) & (lane < (gsel + 1) * epg)
    le = jnp.where(emask, logits, neg)
    m1, e1 = first_max(le)
    m2, e2 = first_max(jnp.where(lane == e1, neg, le))
    ex = jnp.exp(m2 - m1)
    p1 = 1.0 / (1.0 + ex)
    p2 = ex * p1
    onehot = ((lane == e1) | (lane == e2)).astype(BF16)
    rr = lax.broadcasted_iota(jnp.int32, (tm, tm), 0)
    cc = lax.broadcasted_iota(jnp.int32, (tm, tm), 1)
    before = _dot((rr > cc).astype(BF16), onehot) + base_ref[0:1, :]
    rank1 = jnp.sum(jnp.where(lane == e1, before, 0.0), axis=-1, keepdims=True).astype(jnp.int32)
    rank2 = jnp.sum(jnp.where(lane == e2, before, 0.0), axis=-1, keepdims=True).astype(jnp.int32)
    base_ref[0:1, :] = base_ref[0:1, :] + jnp.sum(onehot.astype(F32), axis=0, keepdims=True)
    ri_ref[...] = jnp.where(lane == 0, e1, jnp.where(lane == 1, e2, jnp.where(lane == 2, rank1, jnp.where(lane == 3, rank2, 0))))
    rw_ref[...] = jnp.where(lane == 0, pg_sel * p1, jnp.where(lane == 1, pg_sel * p2, 0.0))
    cnt_ref[...] = base_ref[...]


def _post(x, og, ma, gb, w_out_b, w_o, g2, w_r, b_r, *, n_exp, n_grp):
    t, d = x.shape
    tm = min(TOKEN_TILE, t)
    row = lambda n: pl.BlockSpec((tm, n), lambda i: (i, 0))
    return pl.pallas_call(
        functools.partial(_post_body, n_exp=n_exp, n_grp=n_grp, epg=n_exp // n_grp),
        grid=(t // tm,),
        in_specs=[row(d), row(og.shape[1]), row(d), row(d), _resident(w_out_b.shape), _resident(w_o.shape),
                  _resident(g2.shape), _resident(w_r.shape), _resident(b_r.shape)],
        out_specs=(row(d), pl.BlockSpec((tm * SUBLANES, LANES), lambda i: (i, 0)), row(LANES), row(LANES),
                   pl.BlockSpec((SUBLANES, LANES), lambda i: (0, 0))),
        out_shape=(jax.ShapeDtypeStruct((t, d), F32), jax.ShapeDtypeStruct((t * SUBLANES, LANES), F32),
                   jax.ShapeDtypeStruct((t, LANES), jnp.int32), jax.ShapeDtypeStruct((t, LANES), F32),
                   jax.ShapeDtypeStruct((SUBLANES, LANES), F32)),
        scratch_shapes=[pltpu.VMEM((SUBLANES, LANES), F32)],
        compiler_params=_params(1),
        name="post",
    )(x, og, ma, gb, w_out_b, w_o, g2, w_r, b_r)


ROW_UNROLL = 8
TILE_ROWS = SUBLANES


def _scatter_body(ztile_ref, dest_ref, t_ref, xs_ref, zero_ref, buf_ref, sem, lsem, zsem, *, ts, r, n_zero):
    i = pl.program_id(0)
    n = pl.num_programs(0)
    rows = ts * TILE_ROWS

    def zero_copy(e):
        start = pl.multiple_of(ztile_ref[e] * TILE_ROWS, r * TILE_ROWS)
        return pltpu.make_async_copy(zero_ref, xs_ref.at[pl.ds(start, r * TILE_ROWS)], zsem)

    def load(step, slot):
        return pltpu.make_async_copy(t_ref.at[pl.ds(pl.multiple_of(step * rows, rows), rows)], buf_ref.at[slot], lsem.at[slot])

    def wait_rows(slot):
        for _ in range(2):
            pltpu.make_async_copy(buf_ref.at[slot], xs_ref.at[pl.ds(0, rows)], sem.at[slot]).wait()

    @pl.when(i == 0)
    def _():
        load(0, 0).start()
        zero_ref[...] = jnp.zeros_like(zero_ref)
        for e in range(n_zero):
            @pl.when(ztile_ref[e] >= 0)
            def _():
                zero_copy(e).start()
        for e in range(n_zero):
            @pl.when(ztile_ref[e] >= 0)
            def _():
                zero_copy(e).wait()

    slot = i % 2
    load(i, slot).wait()

    def start(q, c):
        src = buf_ref.at[slot, pl.ds(pl.multiple_of(q * TILE_ROWS, TILE_ROWS), TILE_ROWS)]
        for k in range(2):
            dst = pl.multiple_of(dest_ref[0, 0, 2 * q + k] * TILE_ROWS, TILE_ROWS)
            pltpu.make_async_copy(src, xs_ref.at[pl.ds(dst, TILE_ROWS)], sem.at[slot]).start()
        return c

    lax.fori_loop(0, ts, start, 0, unroll=ROW_UNROLL)

    @pl.when(i > 0)
    def _():
        wait_rows(1 - slot)

    @pl.when(i + 1 < n)
    def _():
        load(i + 1, 1 - slot).start()

    @pl.when(i == n - 1)
    def _():
        wait_rows(slot)


def _scatter(zero_tile_start, dest, t_tiles, n_rows, r):
    n_tok = t_tiles.shape[0] // TILE_ROWS
    ts = min(ROW_TILE, n_tok)
    dest3 = dest.reshape(n_tok // ts, 1, 2 * ts)
    grid_spec = pltpu.PrefetchScalarGridSpec(
        num_scalar_prefetch=1,
        grid=(n_tok // ts,),
        in_specs=[pl.BlockSpec((1, 1, 2 * ts), lambda i, zt: (i, 0, 0), memory_space=pltpu.SMEM),
                  pl.BlockSpec(memory_space=pl.ANY)],
        out_specs=pl.BlockSpec(memory_space=pl.ANY),
        scratch_shapes=[pltpu.VMEM((r * TILE_ROWS, LANES), F32), pltpu.VMEM((2, ts * TILE_ROWS, LANES), F32),
                        pltpu.SemaphoreType.DMA((2,)), pltpu.SemaphoreType.DMA((2,)), pltpu.SemaphoreType.DMA(())],
    )
    return pl.pallas_call(
        functools.partial(_scatter_body, ts=ts, r=r, n_zero=zero_tile_start.shape[0]),
        grid_spec=grid_spec,
        out_shape=jax.ShapeDtypeStruct((n_rows * TILE_ROWS, LANES), F32),
        compiler_params=_params(1),
        name="moe_scatter",
    )(zero_tile_start, dest3, t_tiles)


def _experts_body(te_ref, nt_ref, xs_ref, wg_ref, wu_ref, wd_ref, ys_ref, *, r):
    i = pl.program_id(0)

    @pl.when(i < nt_ref[0])
    def _():
        x = _load_row_tiles(xs_ref, r).astype(BF16)
        hg = _dot(x, wg_ref[...])
        hu = _dot(x, wu_ref[...])
        hdn = (hg * _sigmoid(hg)) * hu
        _store_row_tiles(ys_ref, _dot(hdn.astype(BF16), wd_ref[...]))

    @pl.when(i >= nt_ref[0])
    def _():
        ys_ref[...] = jnp.zeros_like(ys_ref)


def _experts(tile_expert, n_tiles_used, xs_tiles, wg, wu, wd, r):
    rows = xs_tiles.shape[0] // TILE_ROWS
    d, de = wg.shape[1], wg.shape[2]
    used = lambda i, nt: jnp.minimum(i, nt[0] - 1)
    grid_spec = pltpu.PrefetchScalarGridSpec(
        num_scalar_prefetch=2,
        grid=(rows // r,),
        in_specs=[pl.BlockSpec((r * TILE_ROWS, LANES), lambda i, te, nt: (used(i, nt), 0)),
                  pl.BlockSpec((None, d, de), lambda i, te, nt: (te[i], 0, 0)),
                  pl.BlockSpec((None, d, de), lambda i, te, nt: (te[i], 0, 0)),
                  pl.BlockSpec((None, de, d), lambda i, te, nt: (te[i], 0, 0))],
        out_specs=pl.BlockSpec((r * TILE_ROWS, LANES), lambda i, te, nt: (i, 0)),
    )
    return pl.pallas_call(
        functools.partial(_experts_body, r=r),
        grid_spec=grid_spec,
        out_shape=jax.ShapeDtypeStruct(xs_tiles.shape, F32),
        compiler_params=_params(1),
        name="moe_experts",
    )(tile_expert, n_tiles_used, xs_tiles, wg, wu, wd)


def _combine_body(dnext_ref, dfirst_ref, x1_ref, rw_ref, ys_ref, gf_ref, y_ref, buf_ref, sem, *, ts):
    i = pl.program_id(0)
    n = pl.num_programs(0)

    def gather(dref, slot):
        def start(q, c):
            for k in range(2):
                src = pl.multiple_of(dref[0, 0, 2 * q + k] * TILE_ROWS, TILE_ROWS)
                dst = buf_ref.at[slot, k, pl.ds(pl.multiple_of(q * TILE_ROWS, TILE_ROWS), TILE_ROWS)]
                pltpu.make_async_copy(ys_ref.at[pl.ds(src, TILE_ROWS)], dst, sem.at[slot]).start()
            return c
        lax.fori_loop(0, ts, start, 0, unroll=ROW_UNROLL)

    @pl.when(i == 0)
    def _():
        gather(dfirst_ref, 0)

    @pl.when(i + 1 < n)
    def _():
        gather(dnext_ref, (i + 1) % 2)

    slot = i % 2
    for k in range(2):
        pltpu.make_async_copy(ys_ref.at[pl.ds(0, ts * TILE_ROWS)], buf_ref.at[slot, k], sem.at[slot]).wait()
    rw = rw_ref[...]
    acc = rw[:, 0:1] * _load_row_tiles(buf_ref.at[slot, 0], ts) + rw[:, 1:2] * _load_row_tiles(buf_ref.at[slot, 1], ts)
    x2 = x1_ref[...] + acc
    r = lax.rsqrt(jnp.mean(x2 * x2, axis=-1, keepdims=True) + EPS)
    y_ref[...] = x2 * r * gf_ref[...]


def _combine(dest, x1, rw, ys_tiles, gf):
    n_tok, d = x1.shape
    ts = min(ROW_TILE, n_tok)
    n = n_tok // ts
    dest3 = dest.reshape(n, 1, 2 * ts)
    return pl.pallas_call(
        functools.partial(_combine_body, ts=ts),
        grid=(n,),
        in_specs=[pl.BlockSpec((1, 1, 2 * ts), lambda i: (jnp.minimum(i + 1, n - 1), 0, 0), memory_space=pltpu.SMEM),
                  pl.BlockSpec((1, 1, 2 * ts), lambda i: (0, 0, 0), memory_space=pltpu.SMEM),
                  pl.BlockSpec((ts, d), lambda i: (i, 0)),
                  pl.BlockSpec((ts, LANES), lambda i: (i, 0)),
                  pl.BlockSpec(memory_space=pl.ANY),
                  _resident(gf.shape)],
        out_specs=pl.BlockSpec((ts, d), lambda i: (i, 0)),
        out_shape=jax.ShapeDtypeStruct((n_tok, d), F32),
        scratch_shapes=[pltpu.VMEM((2, 2, ts * TILE_ROWS, LANES), F32), pltpu.SemaphoreType.DMA((2,))],
        compiler_params=_params(1),
        name="moe_combine",
    )(dest3, dest3, x1, rw, ys_tiles, gf)


def _expert_tile(n_tok, n_exp):
    r = EXPERT_TILE
    while r > LANES and r > (2 * n_tok) // n_exp:
        r //= 2
    return r


def _moe(x1, t, ri, rw, counts, wg, wu, wd, gf, *, n_exp):
    n_tok, d = x1.shape
    r = _expert_tile(n_tok, n_exp)
    cnt = counts[0, :n_exp].astype(jnp.int32)
    padded = ((cnt + r - 1) // r) * r
    ends = jnp.cumsum(padded)
    offs = ends - padded
    n_tiles = (2 * n_tok) // r + n_exp
    tile_start = jnp.arange(n_tiles, dtype=jnp.int32) * r
    tile_expert = jnp.minimum(jnp.sum(tile_start[:, None] >= ends[None, :], axis=1), n_exp - 1).astype(jnp.int32)
    n_used = (ends[-1] // r).astype(jnp.int32).reshape(1)
    tail = ends[-1] + jnp.arange(n_exp, dtype=jnp.int32) * r
    zero_tile_start = jnp.concatenate([jnp.where(cnt > 0, ends - r, -1),
                                       jnp.where(tail < n_tiles * r, tail, -1)]).astype(jnp.int32)
    dest = jnp.take(offs, ri[:, 0:2], axis=0) + ri[:, 2:4]
    assert d == SUBLANES * LANES, "row-tile layout holds one 1024-wide f32 row per (8, 128) tile"
    xs = _scatter(zero_tile_start, dest, t, n_tiles * r, r)
    ys = _experts(tile_expert, n_used, xs, wg, wu, wd, r)
    return _combine(dest, x1, rw, ys, gf)


def _pad_rows8(c):
    return jnp.pad(c, ((0, 0), (SUBLANES - c.shape[1], 0), (0, 0)))


def _lane_row(v):
    return jnp.pad(v.astype(F32), (0, LANES - v.shape[0])).reshape(1, LANES)


def _run(x, prev_a, prev_qkv, s0, p, dims):
    b, l, d = x.shape
    dc, dq, dv, nh, dk, dvh, n_exp, n_grp = dims
    xf = x.reshape(b * l, d)
    u, bg, qkv, z, ga, gb, ab = _inproj(xf, p["g1"], p["w_main"], p["w_ab"], dc=dc, dq=dq, dv=dv, d=d)
    r3 = lambda a: a.reshape(b, l, a.shape[-1])
    ma = _mixer_a(r3(u), r3(bg), r3(ga), _pad_rows8(prev_a), p["conv_a_w"], p["w_out_a"])
    og, s_fin = _gdn(r3(qkv), r3(ab), r3(z), _pad_rows8(prev_qkv), s0, p["conv_qkv_w"], p["alog"], p["dtb"],
                     p["onorm_g"], nh=nh, dk=dk, dv=dvh)
    x1, t, ri, rw, counts = _post(xf, og.reshape(b * l, dv), ma.reshape(b * l, d), gb, p["w_out_b"], p["w_o"],
                                  p["g2"], p["w_r"], p["b_r"], n_exp=n_exp, n_grp=n_grp)
    y = _moe(x1, t, ri, rw, counts, p["wg"], p["wu"], p["wd"], p["gf"], n_exp=n_exp)
    wa = prev_a.shape[1]
    wq = prev_qkv.shape[1]
    new_a = r3(u)[:, l - wa:, :].astype(F32)
    new_qkv = r3(qkv)[:, l - wq:, :].astype(F32)
    return y.reshape(b, l, d), new_a, new_qkv, s_fin


def kernel(x_prompt, x_sample, cache_conv_a, cache_conv_qkv, state_gdn, norm1_g, w_in, conv_a_w, w_out_a, conv_qkv_w, a_log, dt_bias, onorm_g, w_out_b, w_o, norm2_g, w_router_group, b_router_group, w_router_expert, b_router_expert, w_gate, w_up, w_down, final_g):
    depth = w_in.shape[0]
    assert depth == 1, "single trunk layer"
    d = x_prompt.shape[-1]
    dc = conv_a_w.shape[-1]
    dq = conv_qkv_w.shape[-1]
    nh = a_log.shape[-1]
    dk, dvh = state_gdn.shape[-2], state_gdn.shape[-1]
    dv = nh * dvh
    n_grp = w_router_group.shape[-1]
    n_exp = w_router_expert.shape[-1]
    dims = (dc, dq, dv, nh, dk, dvh, n_exp, n_grp)

    w = w_in[0]
    o = [0, dc, 2 * dc, 3 * dc, 3 * dc + dq, 3 * dc + dq + dv, 3 * dc + dq + dv + nh, 3 * dc + dq + dv + 2 * nh,
         3 * dc + dq + dv + 2 * nh + d, 3 * dc + dq + dv + 2 * nh + 2 * d]
    seg = lambda k: w[:, o[k]:o[k + 1]]
    w_main = jnp.concatenate([seg(0), seg(2), seg(1), seg(3), seg(4), seg(7), seg(8)], axis=1).astype(BF16)
    w_ab = jnp.pad(jnp.concatenate([seg(5), seg(6)], axis=1), ((0, 0), (0, LANES - 2 * nh))).astype(BF16)
    w_r = jnp.pad(jnp.concatenate([w_router_expert[0], w_router_group[0]], axis=1),
                  ((0, 0), (0, LANES - n_exp - n_grp))).astype(BF16)
    b_r = _lane_row(jnp.concatenate([b_router_expert[0], b_router_group[0]]))
    p = dict(
        g1=norm1_g[0].reshape(1, d), w_main=w_main, w_ab=w_ab,
        conv_a_w=conv_a_w[0], w_out_a=w_out_a[0].astype(BF16),
        conv_qkv_w=conv_qkv_w[0], alog=_lane_row(a_log[0]), dtb=_lane_row(dt_bias[0]),
        onorm_g=onorm_g[0].reshape(1, dvh),
        w_out_b=w_out_b[0].astype(BF16), w_o=w_o[0].astype(BF16), g2=norm2_g[0].reshape(1, d),
        w_r=w_r, b_r=b_r,
        wg=w_gate[0].astype(BF16), wu=w_up[0].astype(BF16), wd=w_down[0].astype(BF16),
        gf=final_g.reshape(1, d),
    )
    bp = x_prompt.shape[0]
    dt_ = x_prompt.dtype
    zero_a = jnp.zeros((bp,) + cache_conv_a.shape[2:], dt_)
    zero_qkv = jnp.zeros((bp,) + cache_conv_qkv.shape[2:], dt_)
    zero_s = jnp.zeros((bp,) + state_gdn.shape[2:], dt_)
    y_p, a_p, q_p, s_p = _run(x_prompt, zero_a, zero_qkv, zero_s, p, dims)
    y_s, a_s, q_s, s_s = _run(x_sample, cache_conv_a[0], cache_conv_qkv[0], state_gdn[0], p, dims)
    return (y_p, y_s, a_p[None], q_p[None], s_p[None], a_s[None], q_s[None], s_s[None])
```

```python
import functools

import jax
import jax.numpy as jnp
from jax import lax
from jax.experimental import pallas as pl
from jax.experimental.pallas import tpu as pltpu

F32 = jnp.float32
BF16 = jnp.bfloat16
EPS = 1e-6
CHUNK = 64
LANES = 128
SUBLANES = 8
VMEM_LIMIT_BYTES = 56 * 1024 * 1024
TOKEN_TILE = 512
SUB_TILE = 256
EXPERT_TILE = 512
ROW_TILE = 256


def _params(n_axes):
    return pltpu.CompilerParams(dimension_semantics=("arbitrary",) * n_axes,
                                vmem_limit_bytes=VMEM_LIMIT_BYTES)


def _resident(shape):
    return pl.BlockSpec(shape, lambda *_: (0,) * len(shape), pipeline_mode=pl.Buffered(1))


def _dot(a, b):
    return jnp.dot(a, b, preferred_element_type=F32)


def _sigmoid(x):
    return 1.0 / (1.0 + jnp.exp(-x))


def _store_row_tiles(ref, x):
    n = x.shape[0]
    for c in range(SUBLANES):
        ref[pl.ds(c, n, stride=SUBLANES), :] = x[:, c * LANES:(c + 1) * LANES]


def _load_row_tiles(ref, n):
    return jnp.concatenate([ref[pl.ds(c, n, stride=SUBLANES), :] for c in range(SUBLANES)], axis=1)


def _causal_conv(u, prev8, w_ref, col0):
    width = w_ref.shape[0]
    n, c = u.shape
    tap = lambda j: w_ref[j:j + 1, col0:col0 + c]
    body = tap(width - 1) * u
    for sh in range(1, width):
        body = body + tap(width - 1 - sh) * pltpu.roll(u, sh, 0)
    head_in = jnp.concatenate([prev8, u[0:SUBLANES]], axis=0)
    lo = SUBLANES - (width - 1)
    head = tap(0) * head_in[lo:lo + SUBLANES]
    for j in range(1, width):
        head = head + tap(j) * head_in[lo + j:lo + j + SUBLANES]
    if n == SUBLANES:
        return head
    return jnp.concatenate([head, body[SUBLANES:]], axis=0)


def _inproj_body(x_ref, g_ref, w_ref, wab_ref, ca_ref, cq_ref, cwa_ref, cwq_ref,
                 va_ref, qkvc_ref, z_ref, ga_ref, gb_ref, ab_ref, ta_ref, tq_ref,
                 carry_a, carry_q, *, dc, dq, seg, nseg, tiles_per_seq, cw):
    i = pl.program_id(0)
    starts_seq = (i % tiles_per_seq) == 0
    x = x_ref[...]
    r = lax.rsqrt(jnp.mean(x * x, axis=-1, keepdims=True) + EPS)
    xn = (x * r * g_ref[...]).astype(BF16)

    def mm(lo, n):
        return _dot(xn, w_ref[:, lo:lo + n])

    def conv(val, cache_ref, carry_ref, tail_ref, cw_ref, col0):
        c = val.shape[1]
        outs = []
        for s in range(nseg):
            v = val[s * seg:(s + 1) * seg]
            prev8 = jnp.where(starts_seq, cache_ref[s, :, col0:col0 + c], carry_ref[s, :, col0:col0 + c])
            outs.append(_causal_conv(v, prev8, cw_ref, col0))
            last8 = v[seg - SUBLANES:seg]
            carry_ref[s, :, col0:col0 + c] = last8
            tail_ref[s, :, col0:col0 + c] = last8
        return outs[0] if nseg == 1 else jnp.concatenate(outs, axis=0)

    def mixer_a_chunk(lo):
        u = mm(dc + lo, cw) * mm(lo, cw)
        va_ref[:, lo:lo + cw] = (mm(2 * dc + lo, cw) * conv(u, ca_ref, carry_a, ta_ref, cwa_ref, lo)).astype(BF16)

    def qkv_chunk(lo):
        qc = conv(mm(3 * dc + lo, cw), cq_ref, carry_q, tq_ref, cwq_ref, lo)
        qkvc_ref[:, lo:lo + cw] = (qc * _sigmoid(qc)).astype(BF16)

    def plain_chunk(ref, o, lo):
        ref[:, lo:lo + cw] = mm(o + lo, cw).astype(BF16)

    heavy = [functools.partial(qkv_chunk, lo) for lo in range(0, dq, cw)]
    heavy += [functools.partial(mixer_a_chunk, lo) for lo in range(0, dc, cw)]
    light = []
    o = 3 * dc + dq
    for ref in (z_ref, ga_ref, gb_ref):
        light += [functools.partial(plain_chunk, ref, o, lo) for lo in range(0, ref.shape[1], cw)]
        o += ref.shape[1]
    for k in range(max(len(heavy), len(light))):
        if k < len(heavy):
            heavy[k]()
        if k < len(light):
            light[k]()
    ab_ref[...] = _dot(xn, wab_ref[...])


def _inproj(x, l, g1, w_main, w_ab, cache_a8, cache_q8, conv_a_w, conv_qkv_w, *, dc, dq, dv, d):
    t = x.shape[0]
    tm = min(TOKEN_TILE, t)
    seg = min(tm, l)
    nseg = tm // seg
    tiles_per_seq = l // seg
    n = t // tm
    row = lambda c: pl.BlockSpec((tm, c), lambda i: (i, 0))
    cache = lambda c: pl.BlockSpec((nseg, SUBLANES, c), lambda i: (i // tiles_per_seq, 0, 0))
    tail = lambda c: pl.BlockSpec((nseg, SUBLANES, c), lambda i: (i, 0, 0))
    out_shapes = (
        jax.ShapeDtypeStruct((t, dc), BF16), jax.ShapeDtypeStruct((t, dq), BF16), jax.ShapeDtypeStruct((t, dv), BF16),
        jax.ShapeDtypeStruct((t, d), BF16), jax.ShapeDtypeStruct((t, d), BF16), jax.ShapeDtypeStruct((t, LANES), F32),
        jax.ShapeDtypeStruct((n * nseg, SUBLANES, dc), F32), jax.ShapeDtypeStruct((n * nseg, SUBLANES, dq), F32))
    return pl.pallas_call(
        functools.partial(_inproj_body, dc=dc, dq=dq, seg=seg, nseg=nseg, tiles_per_seq=tiles_per_seq, cw=SUB_TILE),
        grid=(n,),
        in_specs=[row(d), _resident((1, d)), _resident(w_main.shape), _resident(w_ab.shape),
                  cache(dc), cache(dq), _resident(conv_a_w.shape), _resident(conv_qkv_w.shape)],
        out_specs=(row(dc), row(dq), row(dv), row(d), row(d), row(LANES), tail(dc), tail(dq)),
        out_shape=out_shapes,
        scratch_shapes=[pltpu.VMEM((nseg, SUBLANES, dc), F32), pltpu.VMEM((nseg, SUBLANES, dq), F32)],
        compiler_params=_params(1),
        name="inproj",
    )(x, g1, w_main, w_ab, cache_a8, cache_q8, conv_a_w, conv_qkv_w)


def _unit_lower_inverse(mats):
    n = mats[0].shape[0]
    r = lax.broadcasted_iota(jnp.int32, (n, n), 0)
    c = lax.broadcasted_iota(jnp.int32, (n, n), 1)
    eye = (r == c).astype(F32)

    def same_block(size):
        return jnp.bitwise_xor(r, c) < size

    bb = lambda m: m.astype(BF16)
    a8 = [jnp.where(same_block(SUBLANES), a, 0.0) for a in mats]
    p2 = [bb(_dot(bb(a), bb(a))) for a in a8]
    p4 = [bb(_dot(p, p)) for p in p2]
    xs = [eye - a for a in a8]
    xs = [x + _dot(bb(x), p) for x, p in zip(xs, p2)]
    xs = [x + _dot(bb(x), p) for x, p in zip(xs, p4)]
    size = SUBLANES
    while size < n:
        in_pair = same_block(2 * size) & jnp.logical_not(same_block(size))
        ys = [_dot(bb(jnp.where(in_pair, a, 0.0)), bb(x)) for a, x in zip(mats, xs)]
        xs = [x - _dot(bb(x), bb(y)) for x, y in zip(xs, ys)]
        size *= 2
    return xs


def _gdn_body(qkv_ref, ab_ref, z_ref, s0_ref, alog_ref, dtb_ref, og_ref_g,
              og_ref, sfin_ref, s_ref, *, nb, nh, dk, dv):
    j = pl.program_id(1)
    last = pl.num_programs(1) - 1
    n = CHUNK
    bb = lambda m: m.astype(BF16)

    @pl.when(j == 0)
    def _():
        s_ref[...] = s0_ref[...]

    rows = lax.broadcasted_iota(jnp.int32, (n, LANES), 0)
    ri = lax.broadcasted_iota(jnp.int32, (n, n), 0)
    ci = lax.broadcasted_iota(jnp.int32, (n, n), 1)
    incl = ri >= ci
    strict = ri > ci

    qs, ks, vs, bcols, gcols, gls, decays = [], [], [], [], [], [], []
    for s in range(nb):
        xc = qkv_ref[s].astype(F32)
        ab = ab_ref[s]
        beta = _sigmoid(ab)
        a_in = pltpu.roll(ab, LANES - nh, 1) + dtb_ref[...]
        softplus = jnp.maximum(a_in, 0.0) + jnp.log(1.0 + jnp.exp(-jnp.abs(a_in)))
        gc = -jnp.exp(alog_ref[...]) * softplus
        sh = 1
        while sh < n:
            gc = gc + jnp.where(rows >= sh, pltpu.roll(gc, sh, 0), 0.0)
            sh *= 2
        gct = gc.T
        for h in range(nh):
            q = xc[:, h * dk:(h + 1) * dk]
            k = xc[:, nh * dk + h * dk:nh * dk + (h + 1) * dk]
            qs.append(q * lax.rsqrt(jnp.sum(q * q, axis=-1, keepdims=True) + EPS) * (dk ** -0.5))
            ks.append(k * lax.rsqrt(jnp.sum(k * k, axis=-1, keepdims=True) + EPS))
            vs.append(xc[:, 2 * nh * dk + h * dv:2 * nh * dk + (h + 1) * dv])
            bcols.append(beta[:, h:h + 1])
            gcol = gc[:, h:h + 1]
            gcols.append(gcol)
            gls.append(gc[n - 1:n, h:h + 1])
            decays.append(jnp.where(incl, jnp.exp(jnp.where(incl, gcol - gct[h:h + 1, :], 0.0)), 0.0))

    np_ = nb * nh
    qk_kk = [lax.dot_general(bb(jnp.concatenate([qs[p], ks[p]], axis=0)), bb(ks[p]),
                             (((1,), (1,)), ((), ())), preferred_element_type=F32) for p in range(np_)]
    attn = [bb(qk_kk[p][:n] * decays[p]) for p in range(np_)]
    ts = _unit_lower_inverse([jnp.where(strict, bcols[p] * qk_kk[p][n:] * decays[p], 0.0) for p in range(np_)])
    egs = [jnp.exp(g) for g in gcols]
    uw = [_dot(bb(ts[p]), bb(jnp.concatenate([bcols[p] * vs[p], (bcols[p] * egs[p]) * ks[p]], axis=1)))
          for p in range(np_)]
    kdt = [bb((ks[p] * jnp.exp(gls[p] - gcols[p])).T) for p in range(np_)]

    sidx = [(p // nh, p % nh) for p in range(np_)]
    s_old = [s_ref[a, h] for a, h in sidx]
    ws = [_dot(bb(jnp.concatenate([uw[p][:, dv:], qs[p] * egs[p]], axis=0)), bb(s_old[p])) for p in range(np_)]
    v_new = [bb(uw[p][:, :dv] - ws[p][:n]) for p in range(np_)]
    o_in = [_dot(attn[p], v_new[p]) for p in range(np_)]
    ds = [_dot(kdt[p], v_new[p]) for p in range(np_)]
    for p, (a, h) in enumerate(sidx):
        s_ref[a, h] = s_old[p] * jnp.exp(gls[p]) + ds[p]
        o = ws[p][n:] + o_in[p]
        on = o * lax.rsqrt(jnp.mean(o * o, axis=-1, keepdims=True) + EPS) * og_ref_g[...]
        zz = z_ref[a, :, h * dv:(h + 1) * dv].astype(F32)
        og_ref[a, :, h * dv:(h + 1) * dv] = (on * (zz * _sigmoid(zz))).astype(BF16)

    @pl.when(j == last)
    def _():
        sfin_ref[...] = s_ref[...]


GDN_SEQS_PER_STEP = 4


def _gdn(qkvc, ab, z, s0, alog, dtb, onorm_g, *, nh, dk, dv):
    b, l, dq = qkvc.shape
    n = CHUNK
    nb = GDN_SEQS_PER_STEP if b % GDN_SEQS_PER_STEP == 0 else 1
    blk = lambda c: pl.BlockSpec((nb, n, c), lambda i, j: (i, j, 0))
    state = pl.BlockSpec((nb, nh, dk, dv), lambda i, j: (i, 0, 0, 0))
    return pl.pallas_call(
        functools.partial(_gdn_body, nb=nb, nh=nh, dk=dk, dv=dv),
        grid=(b // nb, l // n),
        in_specs=[blk(dq), blk(LANES), blk(nh * dv), state,
                  _resident(alog.shape), _resident(dtb.shape), _resident(onorm_g.shape)],
        out_specs=(blk(nh * dv), state),
        out_shape=(jax.ShapeDtypeStruct((b, l, nh * dv), BF16), jax.ShapeDtypeStruct((b, nh, dk, dv), F32)),
        scratch_shapes=[pltpu.VMEM((nb, nh, dk, dv), F32)],
        compiler_params=_params(2),
        name="gdn",
    )(qkvc, ab, z, s0, alog, dtb, onorm_g)


def _post_body(x_ref, va_ref, og_ref, ga_ref, gb_ref, woa_ref, wob_ref, wo_ref, g2_ref, wr_ref, br_ref,
               x1_ref, t_ref, ri_ref, rw_ref, cnt_ref, base_ref, *, n_exp, n_grp, epg, sub):
    i = pl.program_id(0)
    tm, d = x_ref.shape

    @pl.when(i == 0)
    def _():
        base_ref[...] = jnp.zeros_like(base_ref)

    chunks = [slice(c * sub, (c + 1) * sub) for c in range(d // sub)]

    def merged(cs):
        y_a = _dot(va_ref[...], woa_ref[:, cs])
        y_b = _dot(og_ref[...], wob_ref[:, cs])
        return (_sigmoid(ga_ref[:, cs].astype(F32)) * y_a + _sigmoid(gb_ref[:, cs].astype(F32)) * y_b).astype(BF16)

    mix = None
    pending = None
    for cs in chunks + [None]:
        m = merged(cs) if cs is not None else None
        if pending is not None:
            part = _dot(pending[1], wo_ref[pending[0], :])
            mix = part if mix is None else mix + part
        pending = (cs, m)
    x1 = x_ref[...] + mix
    x1_ref[...] = x1
    r = lax.rsqrt(jnp.mean(x1 * x1, axis=-1, keepdims=True) + EPS)
    t = x1 * r * g2_ref[...]
    _store_row_tiles(t_ref, t)
    logits = _dot(t.astype(BF16), wr_ref[...]) + br_ref[...]
    lane = lax.broadcasted_iota(jnp.int32, (tm, LANES), 1)
    lane_f = lane.astype(F32)
    neg = -jnp.inf

    def first_max(v):
        mx = jnp.max(v, axis=-1, keepdims=True)
        idx = jnp.min(jnp.where(v == mx, lane_f, float(LANES)), axis=-1, keepdims=True)
        return mx, idx.astype(jnp.int32)

    gmask = (lane >= n_exp) & (lane < n_exp + n_grp)
    gmax, gidx = first_max(jnp.where(gmask, logits, neg))
    gsel = gidx - n_exp
    pg_sel = 1.0 / jnp.sum(jnp.where(gmask, jnp.exp(logits - gmax), 0.0), axis=-1, keepdims=True)
    emask = (lane >= gsel * epg) & (lane < (gsel + 1) * epg)
    le = jnp.where(emask, logits, neg)
    m1, e1 = first_max(le)
    m2, e2 = first_max(jnp.where(lane == e1, neg, le))
    ex = jnp.exp(m2 - m1)
    p1 = 1.0 / (1.0 + ex)
    p2 = ex * p1
    onehot = ((lane == e1) | (lane == e2)).astype(BF16)
    rr = lax.broadcasted_iota(jnp.int32, (tm, tm), 0)
    cc = lax.broadcasted_iota(jnp.int32, (tm, tm), 1)
    before = _dot((rr > cc).astype(BF16), onehot) + base_ref[0:1, :]
    rank1 = jnp.sum(jnp.where(lane == e1, before, 0.0), axis=-1, keepdims=True).astype(jnp.int32)
    rank2 = jnp.sum(jnp.where(lane == e2, before, 0.0), axis=-1, keepdims=True).astype(jnp.int32)
    base_ref[0:1, :] = base_ref[0:1, :] + jnp.sum(onehot.astype(F32), axis=0, keepdims=True)
    ri_ref[...] = jnp.where(lane == 0, e1, jnp.where(lane == 1, e2, jnp.where(lane == 2, rank1, jnp.where(lane == 3, rank2, 0))))
    rw_ref[...] = jnp.where(lane == 0, pg_sel * p1, jnp.where(lane == 1, pg_sel * p2, 0.0))
    cnt_ref[...] = base_ref[...]


def _post(x, va, og, ga, gb, w_out_a, w_out_b, w_o, g2, w_r, b_r, *, n_exp, n_grp):
    t, d = x.shape
    tm = min(TOKEN_TILE, t)
    row = lambda n: pl.BlockSpec((tm, n), lambda i: (i, 0))
    return pl.pallas_call(
        functools.partial(_post_body, n_exp=n_exp, n_grp=n_grp, epg=n_exp // n_grp, sub=SUB_TILE),
        grid=(t // tm,),
        in_specs=[row(d), row(va.shape[1]), row(og.shape[1]), row(d), row(d),
                  _resident(w_out_a.shape), _resident(w_out_b.shape), _resident(w_o.shape),
                  _resident(g2.shape), _resident(w_r.shape), _resident(b_r.shape)],
        out_specs=(row(d), pl.BlockSpec((tm * SUBLANES, LANES), lambda i: (i, 0)), row(LANES), row(LANES),
                   pl.BlockSpec((SUBLANES, LANES), lambda i: (0, 0))),
        out_shape=(jax.ShapeDtypeStruct((t, d), F32), jax.ShapeDtypeStruct((t * SUBLANES, LANES), F32),
                   jax.ShapeDtypeStruct((t, LANES), jnp.int32), jax.ShapeDtypeStruct((t, LANES), F32),
                   jax.ShapeDtypeStruct((SUBLANES, LANES), F32)),
        scratch_shapes=[pltpu.VMEM((SUBLANES, LANES), F32)],
        compiler_params=_params(1),
        name="post",
    )(x, va, og, ga, gb, w_out_a, w_out_b, w_o, g2, w_r, b_r)


ROW_UNROLL = 8
TILE_ROWS = SUBLANES


def _scatter_body(ztile_ref, dest_ref, t_ref, xs_ref, zero_ref, buf_ref, sem, lsem, zsem, *, ts, r, n_zero):
    i = pl.program_id(0)
    n = pl.num_programs(0)
    rows = ts * TILE_ROWS

    def zero_copy(e):
        start = pl.multiple_of(ztile_ref[e] * TILE_ROWS, r * TILE_ROWS)
        return pltpu.make_async_copy(zero_ref, xs_ref.at[pl.ds(start, r * TILE_ROWS)], zsem)

    def load(step, slot):
        return pltpu.make_async_copy(t_ref.at[pl.ds(pl.multiple_of(step * rows, rows), rows)], buf_ref.at[slot], lsem.at[slot])

    def wait_rows(slot):
        for _ in range(2):
            pltpu.make_async_copy(buf_ref.at[slot], xs_ref.at[pl.ds(0, rows)], sem.at[slot]).wait()

    @pl.when(i == 0)
    def _():
        load(0, 0).start()
        zero_ref[...] = jnp.zeros_like(zero_ref)
        for e in range(n_zero):
            @pl.when(ztile_ref[e] >= 0)
            def _():
                zero_copy(e).start()
        for e in range(n_zero):
            @pl.when(ztile_ref[e] >= 0)
            def _():
                zero_copy(e).wait()

    slot = i % 2
    load(i, slot).wait()

    def start(q, c):
        src = buf_ref.at[slot, pl.ds(pl.multiple_of(q * TILE_ROWS, TILE_ROWS), TILE_ROWS)]
        for k in range(2):
            dst = pl.multiple_of(dest_ref[0, 0, 2 * q + k] * TILE_ROWS, TILE_ROWS)
            pltpu.make_async_copy(src, xs_ref.at[pl.ds(dst, TILE_ROWS)], sem.at[slot]).start(priority=k)
        return c

    lax.fori_loop(0, ts, start, 0, unroll=ROW_UNROLL)

    @pl.when(i > 0)
    def _():
        wait_rows(1 - slot)

    @pl.when(i + 1 < n)
    def _():
        load(i + 1, 1 - slot).start()

    @pl.when(i == n - 1)
    def _():
        wait_rows(slot)


def _scatter(zero_tile_start, dest, t_tiles, n_rows, r):
    n_tok = t_tiles.shape[0] // TILE_ROWS
    ts = min(ROW_TILE, n_tok)
    dest3 = dest.reshape(n_tok // ts, 1, 2 * ts)
    grid_spec = pltpu.PrefetchScalarGridSpec(
        num_scalar_prefetch=1,
        grid=(n_tok // ts,),
        in_specs=[pl.BlockSpec((1, 1, 2 * ts), lambda i, zt: (i, 0, 0), memory_space=pltpu.SMEM),
                  pl.BlockSpec(memory_space=pl.ANY)],
        out_specs=pl.BlockSpec(memory_space=pl.ANY),
        scratch_shapes=[pltpu.VMEM((r * TILE_ROWS, LANES), F32), pltpu.VMEM((2, ts * TILE_ROWS, LANES), F32),
                        pltpu.SemaphoreType.DMA((2,)), pltpu.SemaphoreType.DMA((2,)), pltpu.SemaphoreType.DMA(())],
    )
    return pl.pallas_call(
        functools.partial(_scatter_body, ts=ts, r=r, n_zero=zero_tile_start.shape[0]),
        grid_spec=grid_spec,
        out_shape=jax.ShapeDtypeStruct((n_rows * TILE_ROWS, LANES), F32),
        compiler_params=_params(1),
        name="moe_scatter",
    )(zero_tile_start, dest3, t_tiles)


def _experts_body(te_ref, nt_ref, xs_ref, wg_ref, wu_ref, wd_ref, ys_ref, *, r, sub):
    i = pl.program_id(0)

    @pl.when(i < nt_ref[0])
    def _():
        wg = wg_ref[...].astype(BF16)
        wu = wu_ref[...].astype(BF16)
        wd = wd_ref[...].astype(BF16)
        views = [pl.ds(s * sub * TILE_ROWS, sub * TILE_ROWS) for s in range(r // sub)]
        xs = [_load_row_tiles(xs_ref.at[v], sub).astype(BF16) for v in views]
        hg = [_dot(x, wg) for x in xs]
        hu = [_dot(x, wu) for x in xs]
        hdn = [((g * _sigmoid(g)) * u).astype(BF16) for g, u in zip(hg, hu)]
        ys = [_dot(h, wd) for h in hdn]
        for v, y in zip(views, ys):
            _store_row_tiles(ys_ref.at[v], y)

    @pl.when(i >= nt_ref[0])
    def _():
        ys_ref[...] = jnp.zeros_like(ys_ref)


def _experts(tile_expert, n_tiles_used, xs_tiles, wg, wu, wd, r):
    rows = xs_tiles.shape[0] // TILE_ROWS
    d, de = wg.shape[1], wg.shape[2]
    used = lambda i, nt: jnp.minimum(i, nt[0] - 1)
    grid_spec = pltpu.PrefetchScalarGridSpec(
        num_scalar_prefetch=2,
        grid=(rows // r,),
        in_specs=[pl.BlockSpec((r * TILE_ROWS, LANES), lambda i, te, nt: (used(i, nt), 0)),
                  pl.BlockSpec((None, d, de), lambda i, te, nt: (te[i], 0, 0)),
                  pl.BlockSpec((None, d, de), lambda i, te, nt: (te[i], 0, 0)),
                  pl.BlockSpec((None, de, d), lambda i, te, nt: (te[i], 0, 0))],
        out_specs=pl.BlockSpec((r * TILE_ROWS, LANES), lambda i, te, nt: (i, 0)),
    )
    return pl.pallas_call(
        functools.partial(_experts_body, r=r, sub=r),
        grid_spec=grid_spec,
        out_shape=jax.ShapeDtypeStruct(xs_tiles.shape, F32),
        compiler_params=_params(1),
        name="moe_experts",
    )(tile_expert, n_tiles_used, xs_tiles, wg, wu, wd)


def _combine_body(dnext_ref, dfirst_ref, x1_ref, rw_ref, ys_ref, gf_ref, y_ref, buf_ref, sem, *, ts):
    i = pl.program_id(0)
    n = pl.num_programs(0)

    def gather(dref, slot):
        def start(q, c):
            for k in range(2):
                src = pl.multiple_of(dref[0, 0, 2 * q + k] * TILE_ROWS, TILE_ROWS)
                dst = buf_ref.at[slot, k, pl.ds(pl.multiple_of(q * TILE_ROWS, TILE_ROWS), TILE_ROWS)]
                pltpu.make_async_copy(ys_ref.at[pl.ds(src, TILE_ROWS)], dst, sem.at[slot]).start(priority=k)
            return c
        lax.fori_loop(0, ts, start, 0, unroll=ROW_UNROLL)

    @pl.when(i == 0)
    def _():
        gather(dfirst_ref, 0)

    @pl.when(i + 1 < n)
    def _():
        gather(dnext_ref, (i + 1) % 2)

    slot = i % 2
    for k in range(2):
        pltpu.make_async_copy(ys_ref.at[pl.ds(0, ts * TILE_ROWS)], buf_ref.at[slot, k], sem.at[slot]).wait()
    rw = rw_ref[...]
    acc = rw[:, 0:1] * _load_row_tiles(buf_ref.at[slot, 0], ts) + rw[:, 1:2] * _load_row_tiles(buf_ref.at[slot, 1], ts)
    x2 = x1_ref[...] + acc
    r = lax.rsqrt(jnp.mean(x2 * x2, axis=-1, keepdims=True) + EPS)
    y_ref[...] = x2 * r * gf_ref[...]


def _combine(dest, x1, rw, ys_tiles, gf):
    n_tok, d = x1.shape
    ts = min(ROW_TILE, n_tok)
    n = n_tok // ts
    dest3 = dest.reshape(n, 1, 2 * ts)
    return pl.pallas_call(
        functools.partial(_combine_body, ts=ts),
        grid=(n,),
        in_specs=[pl.BlockSpec((1, 1, 2 * ts), lambda i: (jnp.minimum(i + 1, n - 1), 0, 0), memory_space=pltpu.SMEM),
                  pl.BlockSpec((1, 1, 2 * ts), lambda i: (0, 0, 0), memory_space=pltpu.SMEM),
                  pl.BlockSpec((ts, d), lambda i: (i, 0)),
                  pl.BlockSpec((ts, LANES), lambda i: (i, 0)),
                  pl.BlockSpec(memory_space=pl.ANY),
                  _resident(gf.shape)],
        out_specs=pl.BlockSpec((ts, d), lambda i: (i, 0)),
        out_shape=jax.ShapeDtypeStruct((n_tok, d), F32),
        scratch_shapes=[pltpu.VMEM((2, 2, ts * TILE_ROWS, LANES), F32), pltpu.SemaphoreType.DMA((2,))],
        compiler_params=_params(1),
        name="moe_combine",
    )(dest3, dest3, x1, rw, ys_tiles, gf)


def _expert_tile(n_tok, n_exp):
    r = EXPERT_TILE
    while r > LANES and r > (2 * n_tok) // n_exp:
        r //= 2
    return r


def _moe(x1, t, ri, rw, counts, wg, wu, wd, gf, *, n_exp):
    n_tok, d = x1.shape
    r = _expert_tile(n_tok, n_exp)
    cnt = counts[0, :n_exp].astype(jnp.int32)
    padded = ((cnt + r - 1) // r) * r
    ends = jnp.cumsum(padded)
    offs = ends - padded
    n_tiles = (2 * n_tok) // r + n_exp
    tile_start = jnp.arange(n_tiles, dtype=jnp.int32) * r
    tile_expert = jnp.minimum(jnp.sum(tile_start[:, None] >= ends[None, :], axis=1), n_exp - 1).astype(jnp.int32)
    n_used = (ends[-1] // r).astype(jnp.int32).reshape(1)
    tail = ends[-1] + jnp.arange(n_exp, dtype=jnp.int32) * r
    zero_tile_start = jnp.concatenate([jnp.where(cnt > 0, ends - r, -1),
                                       jnp.where(tail < n_tiles * r, tail, -1)]).astype(jnp.int32)
    seg_start = jnp.sum(jnp.where(ri[:, 0:2, None] == jnp.arange(n_exp, dtype=jnp.int32), offs, 0), axis=-1)
    dest = seg_start + ri[:, 2:4]
    assert d == SUBLANES * LANES, "row-tile layout holds one 1024-wide f32 row per (8, 128) tile"
    xs = _scatter(zero_tile_start, dest, t, n_tiles * r, r)
    ys = _experts(tile_expert, n_used, xs, wg, wu, wd, r)
    return _combine(dest, x1, rw, ys, gf)


def _pad_rows8(c):
    return jnp.pad(c, ((0, 0), (SUBLANES - c.shape[1], 0), (0, 0)))


def _lane_row(v):
    return jnp.pad(v.astype(F32), (0, LANES - v.shape[0])).reshape(1, LANES)


def _run(x, prev_a, prev_qkv, s0, p, dims):
    b, l, d = x.shape
    dc, dq, dv, nh, dk, dvh, n_exp, n_grp = dims
    xf = x.reshape(b * l, d)
    va, qkvc, z, ga, gb, ab, tail_a, tail_q = _inproj(
        xf, l, p["g1"], p["w_main"], p["w_ab"], _pad_rows8(prev_a), _pad_rows8(prev_qkv),
        p["conv_a_w"], p["conv_qkv_w"], dc=dc, dq=dq, dv=dv, d=d)
    r3 = lambda a: a.reshape(b, l, a.shape[-1])
    og, s_fin = _gdn(r3(qkvc), r3(ab), r3(z), s0, p["alog"], p["dtb"], p["onorm_g"], nh=nh, dk=dk, dv=dvh)
    x1, t, ri, rw, counts = _post(xf, va, og.reshape(b * l, dv), ga, gb, p["w_out_a"], p["w_out_b"], p["w_o"],
                                  p["g2"], p["w_r"], p["b_r"], n_exp=n_exp, n_grp=n_grp)
    y = _moe(x1, t, ri, rw, counts, p["wg"], p["wu"], p["wd"], p["gf"], n_exp=n_exp)
    last = lambda tl, w: tl.reshape(b, -1, SUBLANES, tl.shape[-1])[:, -1, SUBLANES - w:, :]
    return y.reshape(b, l, d), last(tail_a, prev_a.shape[1]), last(tail_q, prev_qkv.shape[1]), s_fin


def kernel(x_prompt, x_sample, cache_conv_a, cache_conv_qkv, state_gdn, norm1_g, w_in, conv_a_w, w_out_a, conv_qkv_w, a_log, dt_bias, onorm_g, w_out_b, w_o, norm2_g, w_router_group, b_router_group, w_router_expert, b_router_expert, w_gate, w_up, w_down, final_g):
    depth = w_in.shape[0]
    assert depth == 1, "single trunk layer"
    d = x_prompt.shape[-1]
    dc = conv_a_w.shape[-1]
    dq = conv_qkv_w.shape[-1]
    nh = a_log.shape[-1]
    dk, dvh = state_gdn.shape[-2], state_gdn.shape[-1]
    dv = nh * dvh
    n_grp = w_router_group.shape[-1]
    n_exp = w_router_expert.shape[-1]
    dims = (dc, dq, dv, nh, dk, dvh, n_exp, n_grp)

    w = w_in[0]
    o = [0, dc, 2 * dc, 3 * dc, 3 * dc + dq, 3 * dc + dq + dv, 3 * dc + dq + dv + nh, 3 * dc + dq + dv + 2 * nh,
         3 * dc + dq + dv + 2 * nh + d, 3 * dc + dq + dv + 2 * nh + 2 * d]
    seg = lambda k: w[:, o[k]:o[k + 1]]
    w_main = jnp.concatenate([seg(0), seg(2), seg(1), seg(3), seg(4), seg(7), seg(8)], axis=1).astype(BF16)
    w_ab = jnp.pad(jnp.concatenate([seg(5), seg(6)], axis=1), ((0, 0), (0, LANES - 2 * nh))).astype(BF16)
    w_r = jnp.pad(jnp.concatenate([w_router_expert[0], w_router_group[0]], axis=1),
                  ((0, 0), (0, LANES - n_exp - n_grp))).astype(BF16)
    b_r = _lane_row(jnp.concatenate([b_router_expert[0], b_router_group[0]]))
    p = dict(
        g1=norm1_g[0].reshape(1, d), w_main=w_main, w_ab=w_ab,
        conv_a_w=conv_a_w[0], w_out_a=w_out_a[0].astype(BF16),
        conv_qkv_w=conv_qkv_w[0], alog=_lane_row(a_log[0]), dtb=_lane_row(dt_bias[0]),
        onorm_g=onorm_g[0].reshape(1, dvh),
        w_out_b=w_out_b[0].astype(BF16), w_o=w_o[0].astype(BF16), g2=norm2_g[0].reshape(1, d),
        w_r=w_r, b_r=b_r,
        wg=w_gate[0], wu=w_up[0], wd=w_down[0],
        gf=final_g.reshape(1, d),
    )
    bp = x_prompt.shape[0]
    dt_ = x_prompt.dtype
    zero_a = jnp.zeros((bp,) + cache_conv_a.shape[2:], dt_)
    zero_qkv = jnp.zeros((bp,) + cache_conv_qkv.shape[2:], dt_)
    zero_s = jnp.zeros((bp,) + state_gdn.shape[2:], dt_)
    y_p, a_p, q_p, s_p = _run(x_prompt, zero_a, zero_qkv, zero_s, p, dims)
    y_s, a_s, q_s, s_s = _run(x_sample, cache_conv_a[0], cache_conv_qkv[0], state_gdn[0], p, dims)
    return (y_p, y_s, a_p[None], q_p[None], s_p[None], a_s[None], q_s[None], s_s[None])
```

```python
import functools

import jax
import jax.numpy as jnp
from jax import lax
from jax.experimental import pallas as pl
from jax.experimental.pallas import tpu as pltpu

F32 = jnp.float32
BF16 = jnp.bfloat16
EPS = 1e-6
CHUNK = 64
LANES = 128
SUBLANES = 8
VMEM_LIMIT_BYTES = 56 * 1024 * 1024
TOKEN_TILE = 512
SUB_TILE = 256
EXPERT_TILE = 512
ROW_TILE = 512


def _params(n_axes):
    return pltpu.CompilerParams(dimension_semantics=("arbitrary",) * n_axes,
                                vmem_limit_bytes=VMEM_LIMIT_BYTES)


def _resident(shape):
    return pl.BlockSpec(shape, lambda *_: (0,) * len(shape), pipeline_mode=pl.Buffered(1))


def _dot(a, b):
    return jnp.dot(a, b, preferred_element_type=F32)


def _sigmoid(x):
    return 1.0 / (1.0 + jnp.exp(-x))


def _store_row_tiles(ref, x):
    n = x.shape[0]
    for c in range(SUBLANES):
        ref[pl.ds(c, n, stride=SUBLANES), :] = x[:, c * LANES:(c + 1) * LANES]


def _load_row_tiles(ref, n):
    return jnp.concatenate([ref[pl.ds(c, n, stride=SUBLANES), :] for c in range(SUBLANES)], axis=1)


def _causal_conv(u, prev8, w_ref, col0):
    width = w_ref.shape[0]
    n, c = u.shape
    tap = lambda j: w_ref[j:j + 1, col0:col0 + c]
    body = tap(width - 1) * u
    for sh in range(1, width):
        body = body + tap(width - 1 - sh) * pltpu.roll(u, sh, 0)
    head_in = jnp.concatenate([prev8, u[0:SUBLANES]], axis=0)
    lo = SUBLANES - (width - 1)
    head = tap(0) * head_in[lo:lo + SUBLANES]
    for j in range(1, width):
        head = head + tap(j) * head_in[lo + j:lo + j + SUBLANES]
    if n == SUBLANES:
        return head
    return jnp.concatenate([head, body[SUBLANES:]], axis=0)


def _inproj_body(x_ref, g_ref, w_ref, wab_ref, ca_ref, cq_ref, cwa_ref, cwq_ref,
                 va_ref, qkvc_ref, z_ref, ga_ref, gb_ref, ab_ref, ta_ref, tq_ref,
                 carry_a, carry_q, *, dc, dq, seg, nseg, tiles_per_seq, cw):
    i = pl.program_id(0)
    starts_seq = (i % tiles_per_seq) == 0
    x = x_ref[...]
    r = lax.rsqrt(jnp.mean(x * x, axis=-1, keepdims=True) + EPS)
    xn = (x * r * g_ref[...]).astype(BF16)

    def mm(lo, n):
        return _dot(xn, w_ref[:, lo:lo + n])

    def conv(val, cache_ref, carry_ref, tail_ref, cw_ref, col0):
        c = val.shape[1]
        outs = []
        for s in range(nseg):
            v = val[s * seg:(s + 1) * seg]
            prev8 = jnp.where(starts_seq, cache_ref[s, :, col0:col0 + c], carry_ref[s, :, col0:col0 + c])
            outs.append(_causal_conv(v, prev8, cw_ref, col0))
            last8 = v[seg - SUBLANES:seg]
            carry_ref[s, :, col0:col0 + c] = last8
            tail_ref[s, :, col0:col0 + c] = last8
        return outs[0] if nseg == 1 else jnp.concatenate(outs, axis=0)

    def mixer_a_task(lo):
        def products():
            return mm(lo, cw), mm(dc + lo, cw), mm(2 * dc + lo, cw)

        def finish(h, c, b):
            va_ref[:, lo:lo + cw] = (b * conv(c * h, ca_ref, carry_a, ta_ref, cwa_ref, lo)).astype(BF16)
        return products, finish

    def qkv_task(lo):
        def products():
            return (mm(3 * dc + lo, cw),)

        def finish(raw):
            qc = conv(raw, cq_ref, carry_q, tq_ref, cwq_ref, lo)
            qkvc_ref[:, lo:lo + cw] = (qc * _sigmoid(qc)).astype(BF16)
        return products, finish

    def plain_task(ref, o, lo):
        def products():
            return (mm(o + lo, cw),)

        def finish(raw):
            ref[:, lo:lo + cw] = raw.astype(BF16)
        return products, finish

    heavy = [qkv_task(lo) for lo in range(0, dq, cw)] + [mixer_a_task(lo) for lo in range(0, dc, cw)]
    light = []
    o = 3 * dc + dq
    for ref in (z_ref, ga_ref, gb_ref):
        light += [plain_task(ref, o, lo) for lo in range(0, ref.shape[1], cw)]
        o += ref.shape[1]
    tasks = []
    for k in range(max(len(heavy), len(light))):
        tasks += heavy[k:k + 1] + light[k:k + 1]
    ready = tasks[0][0]()
    for k, (_, finish) in enumerate(tasks):
        nxt = tasks[k + 1][0]() if k + 1 < len(tasks) else None
        finish(*ready)
        ready = nxt
    ab_ref[...] = _dot(xn, wab_ref[...])


def _inproj(x, l, g1, w_main, w_ab, cache_a8, cache_q8, conv_a_w, conv_qkv_w, *, dc, dq, dv, d):
    t = x.shape[0]
    tm = min(TOKEN_TILE, t)
    seg = min(tm, l)
    nseg = tm // seg
    tiles_per_seq = l // seg
    n = t // tm
    row = lambda c: pl.BlockSpec((tm, c), lambda i: (i, 0))
    cache = lambda c: pl.BlockSpec((nseg, SUBLANES, c), lambda i: (i // tiles_per_seq, 0, 0))
    tail = lambda c: pl.BlockSpec((nseg, SUBLANES, c), lambda i: (i, 0, 0))
    out_shapes = (
        jax.ShapeDtypeStruct((t, dc), BF16), jax.ShapeDtypeStruct((t, dq), BF16), jax.ShapeDtypeStruct((t, dv), BF16),
        jax.ShapeDtypeStruct((t, d), BF16), jax.ShapeDtypeStruct((t, d), BF16), jax.ShapeDtypeStruct((t, LANES), F32),
        jax.ShapeDtypeStruct((n * nseg, SUBLANES, dc), F32), jax.ShapeDtypeStruct((n * nseg, SUBLANES, dq), F32))
    return pl.pallas_call(
        functools.partial(_inproj_body, dc=dc, dq=dq, seg=seg, nseg=nseg, tiles_per_seq=tiles_per_seq, cw=SUB_TILE),
        grid=(n,),
        in_specs=[row(d), _resident((1, d)), _resident(w_main.shape), _resident(w_ab.shape),
                  cache(dc), cache(dq), _resident(conv_a_w.shape), _resident(conv_qkv_w.shape)],
        out_specs=(row(dc), row(dq), row(dv), row(d), row(d), row(LANES), tail(dc), tail(dq)),
        out_shape=out_shapes,
        scratch_shapes=[pltpu.VMEM((nseg, SUBLANES, dc), F32), pltpu.VMEM((nseg, SUBLANES, dq), F32)],
        compiler_params=_params(1),
        name="inproj",
    )(x, g1, w_main, w_ab, cache_a8, cache_q8, conv_a_w, conv_qkv_w)


def _unit_lower_inverse(mats):
    n = mats[0].shape[0]
    r = lax.broadcasted_iota(jnp.int32, (n, n), 0)
    c = lax.broadcasted_iota(jnp.int32, (n, n), 1)
    eye = (r == c).astype(F32)

    def same_block(size):
        return jnp.bitwise_xor(r, c) < size

    bb = lambda m: m.astype(BF16)
    a8 = [jnp.where(same_block(SUBLANES), a, 0.0) for a in mats]
    p2 = [bb(_dot(bb(a), bb(a))) for a in a8]
    p4 = [bb(_dot(p, p)) for p in p2]
    xs = [eye - a for a in a8]
    xs = [x + _dot(bb(x), p) for x, p in zip(xs, p2)]
    xs = [x + _dot(bb(x), p) for x, p in zip(xs, p4)]
    size = SUBLANES
    while size < n:
        in_pair = same_block(2 * size) & jnp.logical_not(same_block(size))
        ys = [_dot(bb(jnp.where(in_pair, a, 0.0)), bb(x)) for a, x in zip(mats, xs)]
        xs = [x - _dot(bb(x), bb(y)) for x, y in zip(xs, ys)]
        size *= 2
    return xs


def _gdn_body(qkv_ref, ab_ref, z_ref, s0_ref, alog_ref, dtb_ref, og_ref_g,
              og_ref, sfin_ref, s_ref, *, nb, nh, dk, dv):
    j = pl.program_id(1)
    last = pl.num_programs(1) - 1
    n = CHUNK
    bb = lambda m: m.astype(BF16)

    @pl.when(j == 0)
    def _():
        s_ref[...] = s0_ref[...]

    rows = lax.broadcasted_iota(jnp.int32, (n, LANES), 0)
    ri = lax.broadcasted_iota(jnp.int32, (n, n), 0)
    ci = lax.broadcasted_iota(jnp.int32, (n, n), 1)
    incl = ri >= ci
    strict = ri > ci

    qs, ks, vs, bcols, gcols, gls, decays = [], [], [], [], [], [], []
    for s in range(nb):
        xc = qkv_ref[s].astype(F32)
        ab = ab_ref[s]
        beta = _sigmoid(ab)
        a_in = pltpu.roll(ab, LANES - nh, 1) + dtb_ref[...]
        softplus = jnp.maximum(a_in, 0.0) + jnp.log(1.0 + jnp.exp(-jnp.abs(a_in)))
        gc = -jnp.exp(alog_ref[...]) * softplus
        sh = 1
        while sh < n:
            gc = gc + jnp.where(rows >= sh, pltpu.roll(gc, sh, 0), 0.0)
            sh *= 2
        gct = gc.T
        for h in range(nh):
            q = xc[:, h * dk:(h + 1) * dk]
            k = xc[:, nh * dk + h * dk:nh * dk + (h + 1) * dk]
            qs.append(q * lax.rsqrt(jnp.sum(q * q, axis=-1, keepdims=True) + EPS) * (dk ** -0.5))
            ks.append(k * lax.rsqrt(jnp.sum(k * k, axis=-1, keepdims=True) + EPS))
            vs.append(xc[:, 2 * nh * dk + h * dv:2 * nh * dk + (h + 1) * dv])
            bcols.append(beta[:, h:h + 1])
            gcol = gc[:, h:h + 1]
            gcols.append(gcol)
            gls.append(gc[n - 1:n, h:h + 1])
            decays.append(jnp.where(incl, jnp.exp(jnp.where(incl, gcol - gct[h:h + 1, :], 0.0)), 0.0))

    np_ = nb * nh
    qk_kk = [lax.dot_general(bb(jnp.concatenate([qs[p], ks[p]], axis=0)), bb(ks[p]),
                             (((1,), (1,)), ((), ())), preferred_element_type=F32) for p in range(np_)]
    attn = [bb(qk_kk[p][:n] * decays[p]) for p in range(np_)]
    ts = _unit_lower_inverse([jnp.where(strict, bcols[p] * qk_kk[p][n:] * decays[p], 0.0) for p in range(np_)])
    egs = [jnp.exp(g) for g in gcols]
    uw = [_dot(bb(ts[p]), bb(jnp.concatenate([bcols[p] * vs[p], (bcols[p] * egs[p]) * ks[p]], axis=1)))
          for p in range(np_)]
    kdt = [bb((ks[p] * jnp.exp(gls[p] - gcols[p])).T) for p in range(np_)]

    sidx = [(p // nh, p % nh) for p in range(np_)]
    s_old = [s_ref[a, h] for a, h in sidx]
    ws = [_dot(bb(jnp.concatenate([uw[p][:, dv:], qs[p] * egs[p]], axis=0)), bb(s_old[p])) for p in range(np_)]
    v_new = [bb(uw[p][:, :dv] - ws[p][:n]) for p in range(np_)]
    o_in = [_dot(attn[p], v_new[p]) for p in range(np_)]
    ds = [_dot(kdt[p], v_new[p]) for p in range(np_)]
    for p, (a, h) in enumerate(sidx):
        s_ref[a, h] = s_old[p] * jnp.exp(gls[p]) + ds[p]
        o = ws[p][n:] + o_in[p]
        on = o * lax.rsqrt(jnp.mean(o * o, axis=-1, keepdims=True) + EPS) * og_ref_g[...]
        zz = z_ref[a, :, h * dv:(h + 1) * dv].astype(F32)
        og_ref[a, :, h * dv:(h + 1) * dv] = (on * (zz * _sigmoid(zz))).astype(BF16)

    @pl.when(j == last)
    def _():
        sfin_ref[...] = s_ref[...]


GDN_SEQS_PER_STEP = 4


def _gdn(qkvc, ab, z, s0, alog, dtb, onorm_g, *, nh, dk, dv):
    b, l, dq = qkvc.shape
    n = CHUNK
    nb = GDN_SEQS_PER_STEP if b % GDN_SEQS_PER_STEP == 0 else 1
    blk = lambda c: pl.BlockSpec((nb, n, c), lambda i, j: (i, j, 0))
    state = pl.BlockSpec((nb, nh, dk, dv), lambda i, j: (i, 0, 0, 0))
    return pl.pallas_call(
        functools.partial(_gdn_body, nb=nb, nh=nh, dk=dk, dv=dv),
        grid=(b // nb, l // n),
        in_specs=[blk(dq), blk(LANES), blk(nh * dv), state,
                  _resident(alog.shape), _resident(dtb.shape), _resident(onorm_g.shape)],
        out_specs=(blk(nh * dv), state),
        out_shape=(jax.ShapeDtypeStruct((b, l, nh * dv), BF16), jax.ShapeDtypeStruct((b, nh, dk, dv), F32)),
        scratch_shapes=[pltpu.VMEM((nb, nh, dk, dv), F32)],
        compiler_params=_params(2),
        name="gdn",
    )(qkvc, ab, z, s0, alog, dtb, onorm_g)


def _post_body(x_ref, va_ref, og_ref, ga_ref, gb_ref, woa_ref, wob_ref, wo_ref, g2_ref, wr_ref, br_ref,
               x1_ref, t_ref, ri_ref, rw_ref, cnt_ref, base_ref, *, n_exp, n_grp, epg, sub):
    i = pl.program_id(0)
    tm, d = x_ref.shape

    @pl.when(i == 0)
    def _():
        base_ref[...] = jnp.zeros_like(base_ref)

    chunks = [slice(c * sub, (c + 1) * sub) for c in range(d // sub)]

    def merged(cs):
        y_a = _dot(va_ref[...], woa_ref[:, cs])
        y_b = _dot(og_ref[...], wob_ref[:, cs])
        return (_sigmoid(ga_ref[:, cs].astype(F32)) * y_a + _sigmoid(gb_ref[:, cs].astype(F32)) * y_b).astype(BF16)

    mix = None
    pending = None
    for cs in chunks + [None]:
        m = merged(cs) if cs is not None else None
        if pending is not None:
            part = _dot(pending[1], wo_ref[pending[0], :])
            mix = part if mix is None else mix + part
        pending = (cs, m)
    x1 = x_ref[...] + mix
    x1_ref[...] = x1
    r = lax.rsqrt(jnp.mean(x1 * x1, axis=-1, keepdims=True) + EPS)
    t = x1 * r * g2_ref[...]
    _store_row_tiles(t_ref, t)
    logits = _dot(t.astype(BF16), wr_ref[...]) + br_ref[...]
    lane = lax.broadcasted_iota(jnp.int32, (tm, LANES), 1)
    lane_f = lane.astype(F32)
    neg = -jnp.inf

    def first_max(v):
        mx = jnp.max(v, axis=-1, keepdims=True)
        idx = jnp.min(jnp.where(v == mx, lane_f, float(LANES)), axis=-1, keepdims=True)
        return mx, idx.astype(jnp.int32)

    gmask = (lane >= n_exp) & (lane < n_exp + n_grp)
    gmax, gidx = first_max(jnp.where(gmask, logits, neg))
    gsel = gidx - n_exp
    pg_sel = 1.0 / jnp.sum(jnp.where(gmask, jnp.exp(logits - gmax), 0.0), axis=-1, keepdims=True)
    emask = (lane >= gsel * epg) & (lane < (gsel + 1) * epg)
    le = jnp.where(emask, logits, neg)
    m1, e1 = first_max(le)
    m2, e2 = first_max(jnp.where(lane == e1, neg, le))
    ex = jnp.exp(m2 - m1)
    p1 = 1.0 / (1.0 + ex)
    p2 = ex * p1
    onehot = ((lane == e1) | (lane == e2)).astype(BF16)
    rr = lax.broadcasted_iota(jnp.int32, (tm, tm), 0)
    cc = lax.broadcasted_iota(jnp.int32, (tm, tm), 1)
    before = _dot((rr > cc).astype(BF16), onehot) + base_ref[0:1, :]
    rank1 = jnp.sum(jnp.where(lane == e1, before, 0.0), axis=-1, keepdims=True).astype(jnp.int32)
    rank2 = jnp.sum(jnp.where(lane == e2, before, 0.0), axis=-1, keepdims=True).astype(jnp.int32)
    base_ref[0:1, :] = base_ref[0:1, :] + jnp.sum(onehot.astype(F32), axis=0, keepdims=True)
    ri_ref[...] = jnp.where(lane == 0, e1, jnp.where(lane == 1, e2, jnp.where(lane == 2, rank1, jnp.where(lane == 3, rank2, 0))))
    rw_ref[...] = jnp.where(lane == 0, pg_sel * p1, jnp.where(lane == 1, pg_sel * p2, 0.0))
    cnt_ref[...] = base_ref[...]


def _post(x, va, og, ga, gb, w_out_a, w_out_b, w_o, g2, w_r, b_r, *, n_exp, n_grp):
    t, d = x.shape
    tm = min(TOKEN_TILE, t)
    row = lambda n: pl.BlockSpec((tm, n), lambda i: (i, 0))
    return pl.pallas_call(
        functools.partial(_post_body, n_exp=n_exp, n_grp=n_grp, epg=n_exp // n_grp, sub=SUB_TILE),
        grid=(t // tm,),
        in_specs=[row(d), row(va.shape[1]), row(og.shape[1]), row(d), row(d),
                  _resident(w_out_a.shape), _resident(w_out_b.shape), _resident(w_o.shape),
                  _resident(g2.shape), _resident(w_r.shape), _resident(b_r.shape)],
        out_specs=(row(d), pl.BlockSpec((tm * SUBLANES, LANES), lambda i: (i, 0)), row(LANES), row(LANES),
                   pl.BlockSpec((SUBLANES, LANES), lambda i: (0, 0))),
        out_shape=(jax.ShapeDtypeStruct((t, d), F32), jax.ShapeDtypeStruct((t * SUBLANES, LANES), F32),
                   jax.ShapeDtypeStruct((t, LANES), jnp.int32), jax.ShapeDtypeStruct((t, LANES), F32),
                   jax.ShapeDtypeStruct((SUBLANES, LANES), F32)),
        scratch_shapes=[pltpu.VMEM((SUBLANES, LANES), F32)],
        compiler_params=_params(1),
        name="post",
    )(x, va, og, ga, gb, w_out_a, w_out_b, w_o, g2, w_r, b_r)


ROW_UNROLL = 8
TILE_ROWS = SUBLANES


def _scatter_body(ztile_ref, dest_ref, t_ref, xs_ref, zero_ref, buf_ref, sem, lsem, zsem, *, ts, r, n_zero):
    i = pl.program_id(0)
    n = pl.num_programs(0)
    rows = ts * TILE_ROWS

    def zero_copy(e):
        start = pl.multiple_of(ztile_ref[e] * TILE_ROWS, r * TILE_ROWS)
        return pltpu.make_async_copy(zero_ref, xs_ref.at[pl.ds(start, r * TILE_ROWS)], zsem)

    def load(step, slot):
        return pltpu.make_async_copy(t_ref.at[pl.ds(pl.multiple_of(step * rows, rows), rows)], buf_ref.at[slot], lsem.at[slot])

    def wait_rows(slot):
        for _ in range(2):
            pltpu.make_async_copy(buf_ref.at[slot], xs_ref.at[pl.ds(0, rows)], sem.at[slot]).wait()

    @pl.when(i == 0)
    def _():
        load(0, 0).start()
        zero_ref[...] = jnp.zeros_like(zero_ref)
        for e in range(n_zero):
            @pl.when(ztile_ref[e] >= 0)
            def _():
                zero_copy(e).start()
        for e in range(n_zero):
            @pl.when(ztile_ref[e] >= 0)
            def _():
                zero_copy(e).wait()

    slot = i % 2
    load(i, slot).wait()

    def start(q, c):
        src = buf_ref.at[slot, pl.ds(pl.multiple_of(q * TILE_ROWS, TILE_ROWS), TILE_ROWS)]
        for k in range(2):
            dst = pl.multiple_of(dest_ref[0, 0, 2 * q + k] * TILE_ROWS, TILE_ROWS)
            pltpu.make_async_copy(src, xs_ref.at[pl.ds(dst, TILE_ROWS)], sem.at[slot]).start(priority=k)
        return c

    lax.fori_loop(0, ts, start, 0, unroll=ROW_UNROLL)

    @pl.when(i > 0)
    def _():
        wait_rows(1 - slot)

    @pl.when(i + 1 < n)
    def _():
        load(i + 1, 1 - slot).start()

    @pl.when(i == n - 1)
    def _():
        wait_rows(slot)


def _scatter(zero_tile_start, dest, t_tiles, n_rows, r):
    n_tok = t_tiles.shape[0] // TILE_ROWS
    ts = min(ROW_TILE, n_tok)
    dest3 = dest.reshape(n_tok // ts, 1, 2 * ts)
    grid_spec = pltpu.PrefetchScalarGridSpec(
        num_scalar_prefetch=1,
        grid=(n_tok // ts,),
        in_specs=[pl.BlockSpec((1, 1, 2 * ts), lambda i, zt: (i, 0, 0), memory_space=pltpu.SMEM),
                  pl.BlockSpec(memory_space=pl.ANY)],
        out_specs=pl.BlockSpec(memory_space=pl.ANY),
        scratch_shapes=[pltpu.VMEM((r * TILE_ROWS, LANES), F32), pltpu.VMEM((2, ts * TILE_ROWS, LANES), F32),
                        pltpu.SemaphoreType.DMA((2,)), pltpu.SemaphoreType.DMA((2,)), pltpu.SemaphoreType.DMA(())],
    )
    return pl.pallas_call(
        functools.partial(_scatter_body, ts=ts, r=r, n_zero=zero_tile_start.shape[0]),
        grid_spec=grid_spec,
        out_shape=jax.ShapeDtypeStruct((n_rows * TILE_ROWS, LANES), F32),
        compiler_params=_params(1),
        name="moe_scatter",
    )(zero_tile_start, dest3, t_tiles)


def _experts_body(te_ref, nt_ref, xs_ref, wg_ref, wu_ref, wd_ref, ys_ref, *, r, sub):
    i = pl.program_id(0)

    @pl.when(i < nt_ref[0])
    def _():
        wg = wg_ref[...].astype(BF16)
        wu = wu_ref[...].astype(BF16)
        wd = wd_ref[...].astype(BF16)
        views = [pl.ds(s * sub * TILE_ROWS, sub * TILE_ROWS) for s in range(r // sub)]
        xs = [_load_row_tiles(xs_ref.at[v], sub).astype(BF16) for v in views]
        hg = [_dot(x, wg) for x in xs]
        hu = [_dot(x, wu) for x in xs]
        hdn = [((g * _sigmoid(g)) * u).astype(BF16) for g, u in zip(hg, hu)]
        ys = [_dot(h, wd) for h in hdn]
        for v, y in zip(views, ys):
            _store_row_tiles(ys_ref.at[v], y)

    @pl.when(i >= nt_ref[0])
    def _():
        ys_ref[...] = jnp.zeros_like(ys_ref)


def _experts(tile_expert, n_tiles_used, xs_tiles, wg, wu, wd, r):
    rows = xs_tiles.shape[0] // TILE_ROWS
    d, de = wg.shape[1], wg.shape[2]
    used = lambda i, nt: jnp.minimum(i, nt[0] - 1)
    grid_spec = pltpu.PrefetchScalarGridSpec(
        num_scalar_prefetch=2,
        grid=(rows // r,),
        in_specs=[pl.BlockSpec((r * TILE_ROWS, LANES), lambda i, te, nt: (used(i, nt), 0)),
                  pl.BlockSpec((None, d, de), lambda i, te, nt: (te[i], 0, 0)),
                  pl.BlockSpec((None, d, de), lambda i, te, nt: (te[i], 0, 0)),
                  pl.BlockSpec((None, de, d), lambda i, te, nt: (te[i], 0, 0))],
        out_specs=pl.BlockSpec((r * TILE_ROWS, LANES), lambda i, te, nt: (i, 0)),
    )
    return pl.pallas_call(
        functools.partial(_experts_body, r=r, sub=r),
        grid_spec=grid_spec,
        out_shape=jax.ShapeDtypeStruct(xs_tiles.shape, F32),
        compiler_params=_params(1),
        name="moe_experts",
    )(tile_expert, n_tiles_used, xs_tiles, wg, wu, wd)


def _combine_body(dnext_ref, dfirst_ref, x1_ref, rw_ref, ys_ref, gf_ref, y_ref, buf_ref, sem, *, ts):
    i = pl.program_id(0)
    n = pl.num_programs(0)

    def gather(dref, slot):
        def start(q, c):
            for k in range(2):
                src = pl.multiple_of(dref[0, 0, 2 * q + k] * TILE_ROWS, TILE_ROWS)
                dst = buf_ref.at[slot, k, pl.ds(pl.multiple_of(q * TILE_ROWS, TILE_ROWS), TILE_ROWS)]
                pltpu.make_async_copy(ys_ref.at[pl.ds(src, TILE_ROWS)], dst, sem.at[slot]).start(priority=k)
            return c
        lax.fori_loop(0, ts, start, 0, unroll=ROW_UNROLL)

    @pl.when(i == 0)
    def _():
        gather(dfirst_ref, 0)

    @pl.when(i + 1 < n)
    def _():
        gather(dnext_ref, (i + 1) % 2)

    slot = i % 2
    for k in range(2):
        pltpu.make_async_copy(ys_ref.at[pl.ds(0, ts * TILE_ROWS)], buf_ref.at[slot, k], sem.at[slot]).wait()
    rw = rw_ref[...]
    acc = rw[:, 0:1] * _load_row_tiles(buf_ref.at[slot, 0], ts) + rw[:, 1:2] * _load_row_tiles(buf_ref.at[slot, 1], ts)
    x2 = x1_ref[...] + acc
    r = lax.rsqrt(jnp.mean(x2 * x2, axis=-1, keepdims=True) + EPS)
    y_ref[...] = x2 * r * gf_ref[...]


def _combine(dest, x1, rw, ys_tiles, gf):
    n_tok, d = x1.shape
    ts = min(ROW_TILE, n_tok)
    n = n_tok // ts
    dest3 = dest.reshape(n, 1, 2 * ts)
    return pl.pallas_call(
        functools.partial(_combine_body, ts=ts),
        grid=(n,),
        in_specs=[pl.BlockSpec((1, 1, 2 * ts), lambda i: (jnp.minimum(i + 1, n - 1), 0, 0), memory_space=pltpu.SMEM),
                  pl.BlockSpec((1, 1, 2 * ts), lambda i: (0, 0, 0), memory_space=pltpu.SMEM),
                  pl.BlockSpec((ts, d), lambda i: (i, 0)),
                  pl.BlockSpec((ts, LANES), lambda i: (i, 0)),
                  pl.BlockSpec(memory_space=pl.ANY),
                  _resident(gf.shape)],
        out_specs=pl.BlockSpec((ts, d), lambda i: (i, 0)),
        out_shape=jax.ShapeDtypeStruct((n_tok, d), F32),
        scratch_shapes=[pltpu.VMEM((2, 2, ts * TILE_ROWS, LANES), F32), pltpu.SemaphoreType.DMA((2,))],
        compiler_params=_params(1),
        name="moe_combine",
    )(dest3, dest3, x1, rw, ys_tiles, gf)


def _expert_tile(n_tok, n_exp):
    r = EXPERT_TILE
    while r > LANES and r > (2 * n_tok) // n_exp:
        r //= 2
    return r


def _moe(x1, t, ri, rw, counts, wg, wu, wd, gf, *, n_exp):
    n_tok, d = x1.shape
    r = _expert_tile(n_tok, n_exp)
    cnt = counts[0, :n_exp].astype(jnp.int32)
    padded = ((cnt + r - 1) // r) * r
    ends = jnp.cumsum(padded)
    offs = ends - padded
    n_tiles = (2 * n_tok) // r + n_exp
    tile_start = jnp.arange(n_tiles, dtype=jnp.int32) * r
    tile_expert = jnp.minimum(jnp.sum(tile_start[:, None] >= ends[None, :], axis=1), n_exp - 1).astype(jnp.int32)
    n_used = (ends[-1] // r).astype(jnp.int32).reshape(1)
    tail = ends[-1] + jnp.arange(n_exp, dtype=jnp.int32) * r
    zero_tile_start = jnp.concatenate([jnp.where(cnt > 0, ends - r, -1),
                                       jnp.where(tail < n_tiles * r, tail, -1)]).astype(jnp.int32)
    seg_start = jnp.sum(jnp.where(ri[:, 0:2, None] == jnp.arange(n_exp, dtype=jnp.int32), offs, 0), axis=-1)
    dest = seg_start + ri[:, 2:4]
    assert d == SUBLANES * LANES, "row-tile layout holds one 1024-wide f32 row per (8, 128) tile"
    xs = _scatter(zero_tile_start, dest, t, n_tiles * r, r)
    ys = _experts(tile_expert, n_used, xs, wg, wu, wd, r)
    return _combine(dest, x1, rw, ys, gf)


def _pad_rows8(c):
    return jnp.pad(c, ((0, 0), (SUBLANES - c.shape[1], 0), (0, 0)))


def _lane_row(v):
    return jnp.pad(v.astype(F32), (0, LANES - v.shape[0])).reshape(1, LANES)


def _run(x, prev_a, prev_qkv, s0, p, dims):
    b, l, d = x.shape
    dc, dq, dv, nh, dk, dvh, n_exp, n_grp = dims
    xf = x.reshape(b * l, d)
    va, qkvc, z, ga, gb, ab, tail_a, tail_q = _inproj(
        xf, l, p["g1"], p["w_main"], p["w_ab"], _pad_rows8(prev_a), _pad_rows8(prev_qkv),
        p["conv_a_w"], p["conv_qkv_w"], dc=dc, dq=dq, dv=dv, d=d)
    r3 = lambda a: a.reshape(b, l, a.shape[-1])
    og, s_fin = _gdn(r3(qkvc), r3(ab), r3(z), s0, p["alog"], p["dtb"], p["onorm_g"], nh=nh, dk=dk, dv=dvh)
    x1, t, ri, rw, counts = _post(xf, va, og.reshape(b * l, dv), ga, gb, p["w_out_a"], p["w_out_b"], p["w_o"],
                                  p["g2"], p["w_r"], p["b_r"], n_exp=n_exp, n_grp=n_grp)
    y = _moe(x1, t, ri, rw, counts, p["wg"], p["wu"], p["wd"], p["gf"], n_exp=n_exp)
    last = lambda tl, w: tl.reshape(b, -1, SUBLANES, tl.shape[-1])[:, -1, SUBLANES - w:, :]
    return y.reshape(b, l, d), last(tail_a, prev_a.shape[1]), last(tail_q, prev_qkv.shape[1]), s_fin


def kernel(x_prompt, x_sample, cache_conv_a, cache_conv_qkv, state_gdn, norm1_g, w_in, conv_a_w, w_out_a, conv_qkv_w, a_log, dt_bias, onorm_g, w_out_b, w_o, norm2_g, w_router_group, b_router_group, w_router_expert, b_router_expert, w_gate, w_up, w_down, final_g):
    depth = w_in.shape[0]
    assert depth == 1, "single trunk layer"
    d = x_prompt.shape[-1]
    dc = conv_a_w.shape[-1]
    dq = conv_qkv_w.shape[-1]
    nh = a_log.shape[-1]
    dk, dvh = state_gdn.shape[-2], state_gdn.shape[-1]
    dv = nh * dvh
    n_grp = w_router_group.shape[-1]
    n_exp = w_router_expert.shape[-1]
    dims = (dc, dq, dv, nh, dk, dvh, n_exp, n_grp)

    w = w_in[0]
    o = [0, dc, 2 * dc, 3 * dc, 3 * dc + dq, 3 * dc + dq + dv, 3 * dc + dq + dv + nh, 3 * dc + dq + dv + 2 * nh,
         3 * dc + dq + dv + 2 * nh + d, 3 * dc + dq + dv + 2 * nh + 2 * d]
    seg = lambda k: w[:, o[k]:o[k + 1]]
    w_main = jnp.concatenate([seg(0), seg(2), seg(1), seg(3), seg(4), seg(7), seg(8)], axis=1).astype(BF16)
    w_ab = jnp.pad(jnp.concatenate([seg(5), seg(6)], axis=1), ((0, 0), (0, LANES - 2 * nh))).astype(BF16)
    w_r = jnp.pad(jnp.concatenate([w_router_expert[0], w_router_group[0]], axis=1),
                  ((0, 0), (0, LANES - n_exp - n_grp))).astype(BF16)
    b_r = _lane_row(jnp.concatenate([b_router_expert[0], b_router_group[0]]))
    p = dict(
        g1=norm1_g[0].reshape(1, d), w_main=w_main, w_ab=w_ab,
        conv_a_w=conv_a_w[0], w_out_a=w_out_a[0].astype(BF16),
        conv_qkv_w=conv_qkv_w[0], alog=_lane_row(a_log[0]), dtb=_lane_row(dt_bias[0]),
        onorm_g=onorm_g[0].reshape(1, dvh),
        w_out_b=w_out_b[0].astype(BF16), w_o=w_o[0].astype(BF16), g2=norm2_g[0].reshape(1, d),
        w_r=w_r, b_r=b_r,
        wg=w_gate[0], wu=w_up[0], wd=w_down[0],
        gf=final_g.reshape(1, d),
    )
    bp = x_prompt.shape[0]
    dt_ = x_prompt.dtype
    zero_a = jnp.zeros((bp,) + cache_conv_a.shape[2:], dt_)
    zero_qkv = jnp.zeros((bp,) + cache_conv_qkv.shape[2:], dt_)
    zero_s = jnp.zeros((bp,) + state_gdn.shape[2:], dt_)
    y_p, a_p, q_p, s_p = _run(x_prompt, zero_a, zero_qkv, zero_s, p, dims)
    y_s, a_s, q_s, s_s = _run(x_sample, cache_conv_a[0], cache_conv_qkv[0], state_gdn[0], p, dims)
    return (y_p, y_s, a_p[None], q_p[None], s_p[None], a_s[None], q_s[None], s_s[None])
```

```python
import functools

import jax
import jax.numpy as jnp
from jax import lax
from jax.experimental import pallas as pl
from jax.experimental.pallas import tpu as pltpu

F32 = jnp.float32
BF16 = jnp.bfloat16
EPS = 1e-6
CHUNK = 64
LANES = 128
SUBLANES = 8
VMEM_LIMIT_BYTES = 56 * 1024 * 1024
TOKEN_TILE = 512
SUB_TILE = 256
EXPERT_TILE = 512
SCATTER_TILE = 1024
GATHER_TILE = 256


def _params(n_axes):
    return pltpu.CompilerParams(dimension_semantics=("arbitrary",) * n_axes,
                                vmem_limit_bytes=VMEM_LIMIT_BYTES)


def _resident(shape):
    return pl.BlockSpec(shape, lambda *_: (0,) * len(shape), pipeline_mode=pl.Buffered(1))


def _dot(a, b):
    return jnp.dot(a, b, preferred_element_type=F32)


def _sigmoid(x):
    return 0.5 * jnp.tanh(0.5 * x) + 0.5


def _silu(x):
    h = 0.5 * x
    return h * (1.0 + jnp.tanh(h))


def _store_row_tiles(ref, x):
    n = x.shape[0]
    for c in range(SUBLANES):
        ref[pl.ds(c, n, stride=SUBLANES), :] = x[:, c * LANES:(c + 1) * LANES]


def _load_row_tiles(ref, n):
    return jnp.concatenate([ref[pl.ds(c, n, stride=SUBLANES), :] for c in range(SUBLANES)], axis=1)


def _causal_conv(u, prev8, w_ref, col0):
    width = w_ref.shape[0]
    n, c = u.shape
    tap = lambda j: w_ref[j:j + 1, col0:col0 + c]
    body = tap(width - 1) * u
    for sh in range(1, width):
        body = body + tap(width - 1 - sh) * pltpu.roll(u, sh, 0)
    head_in = jnp.concatenate([prev8, u[0:SUBLANES]], axis=0)
    lo = SUBLANES - (width - 1)
    head = tap(0) * head_in[lo:lo + SUBLANES]
    for j in range(1, width):
        head = head + tap(j) * head_in[lo + j:lo + j + SUBLANES]
    if n == SUBLANES:
        return head
    return jnp.concatenate([head, body[SUBLANES:]], axis=0)


def _inproj_body(x_ref, g_ref, w_ref, wab_ref, ca_ref, cq_ref, cwa_ref, cwq_ref,
                 va_ref, qkvc_ref, z_ref, ga_ref, gb_ref, ab_ref, ta_ref, tq_ref,
                 carry_a, carry_q, *, dc, dq, seg, nseg, tiles_per_seq, cw):
    i = pl.program_id(0)
    starts_seq = (i % tiles_per_seq) == 0
    x = x_ref[...]
    r = lax.rsqrt(jnp.mean(x * x, axis=-1, keepdims=True) + EPS)
    xn = (x * r * g_ref[...]).astype(BF16)

    def mm(lo, n):
        return _dot(xn, w_ref[:, lo:lo + n])

    def conv(val, cache_ref, carry_ref, tail_ref, cw_ref, col0):
        c = val.shape[1]
        outs = []
        for s in range(nseg):
            v = val[s * seg:(s + 1) * seg]
            prev8 = jnp.where(starts_seq, cache_ref[s, :, col0:col0 + c], carry_ref[s, :, col0:col0 + c])
            outs.append(_causal_conv(v, prev8, cw_ref, col0))
            last8 = v[seg - SUBLANES:seg]
            carry_ref[s, :, col0:col0 + c] = last8
            tail_ref[s, :, col0:col0 + c] = last8
        return outs[0] if nseg == 1 else jnp.concatenate(outs, axis=0)

    def mixer_a_task(lo):
        def products():
            return mm(lo, cw), mm(dc + lo, cw), mm(2 * dc + lo, cw)

        def finish(h, c, b):
            va_ref[:, lo:lo + cw] = (b * conv(c * h, ca_ref, carry_a, ta_ref, cwa_ref, lo)).astype(BF16)
        return products, finish

    def qkv_task(lo):
        def products():
            return (mm(3 * dc + lo, cw),)

        def finish(raw):
            qc = conv(raw, cq_ref, carry_q, tq_ref, cwq_ref, lo)
            qkvc_ref[:, lo:lo + cw] = _silu(qc).astype(BF16)
        return products, finish

    def plain_task(ref, o, lo):
        def products():
            return (mm(o + lo, cw),)

        def finish(raw):
            ref[:, lo:lo + cw] = raw.astype(BF16)
        return products, finish

    heavy = [qkv_task(lo) for lo in range(0, dq, cw)] + [mixer_a_task(lo) for lo in range(0, dc, cw)]
    light = []
    o = 3 * dc + dq
    for ref in (z_ref, ga_ref, gb_ref):
        light += [plain_task(ref, o, lo) for lo in range(0, ref.shape[1], cw)]
        o += ref.shape[1]
    tasks = []
    for k in range(max(len(heavy), len(light))):
        tasks += heavy[k:k + 1] + light[k:k + 1]
    ready = tasks[0][0]()
    for k, (_, finish) in enumerate(tasks):
        nxt = tasks[k + 1][0]() if k + 1 < len(tasks) else None
        finish(*ready)
        ready = nxt
    ab_ref[...] = _dot(xn, wab_ref[...])


def _inproj(x, l, g1, w_main, w_ab, cache_a8, cache_q8, conv_a_w, conv_qkv_w, *, dc, dq, dv, d):
    t = x.shape[0]
    tm = min(TOKEN_TILE, t)
    seg = min(tm, l)
    nseg = tm // seg
    tiles_per_seq = l // seg
    n = t // tm
    row = lambda c: pl.BlockSpec((tm, c), lambda i: (i, 0))
    cache = lambda c: pl.BlockSpec((nseg, SUBLANES, c), lambda i: (i // tiles_per_seq, 0, 0))
    tail = lambda c: pl.BlockSpec((nseg, SUBLANES, c), lambda i: (i, 0, 0))
    out_shapes = (
        jax.ShapeDtypeStruct((t, dc), BF16), jax.ShapeDtypeStruct((t, dq), BF16), jax.ShapeDtypeStruct((t, dv), BF16),
        jax.ShapeDtypeStruct((t, d), BF16), jax.ShapeDtypeStruct((t, d), BF16), jax.ShapeDtypeStruct((t, LANES), F32),
        jax.ShapeDtypeStruct((n * nseg, SUBLANES, dc), F32), jax.ShapeDtypeStruct((n * nseg, SUBLANES, dq), F32))
    return pl.pallas_call(
        functools.partial(_inproj_body, dc=dc, dq=dq, seg=seg, nseg=nseg, tiles_per_seq=tiles_per_seq, cw=SUB_TILE),
        grid=(n,),
        in_specs=[row(d), _resident((1, d)), _resident(w_main.shape), _resident(w_ab.shape),
                  cache(dc), cache(dq), _resident(conv_a_w.shape), _resident(conv_qkv_w.shape)],
        out_specs=(row(dc), row(dq), row(dv), row(d), row(d), row(LANES), tail(dc), tail(dq)),
        out_shape=out_shapes,
        scratch_shapes=[pltpu.VMEM((nseg, SUBLANES, dc), F32), pltpu.VMEM((nseg, SUBLANES, dq), F32)],
        compiler_params=_params(1),
        name="inproj",
    )(x, g1, w_main, w_ab, cache_a8, cache_q8, conv_a_w, conv_qkv_w)


def _unit_lower_inverse(mats):
    n = mats[0].shape[0]
    r = lax.broadcasted_iota(jnp.int32, (n, n), 0)
    c = lax.broadcasted_iota(jnp.int32, (n, n), 1)
    eye = (r == c).astype(F32)

    def same_block(size):
        return jnp.bitwise_xor(r, c) < size

    bb = lambda m: m.astype(BF16)
    a8 = [jnp.where(same_block(SUBLANES), a, 0.0) for a in mats]
    p2 = [bb(_dot(bb(a), bb(a))) for a in a8]
    p4 = [bb(_dot(p, p)) for p in p2]
    xs = [eye - a for a in a8]
    xs = [x + _dot(bb(x), p) for x, p in zip(xs, p2)]
    xs = [x + _dot(bb(x), p) for x, p in zip(xs, p4)]
    size = SUBLANES
    while size < n:
        in_pair = same_block(2 * size) & jnp.logical_not(same_block(size))
        ys = [_dot(bb(jnp.where(in_pair, a, 0.0)), bb(x)) for a, x in zip(mats, xs)]
        xs = [x - _dot(bb(x), bb(y)) for x, y in zip(xs, ys)]
        size *= 2
    return xs


def _gdn_body(qkv_ref, ab_ref, z_ref, s0_ref, alog_ref, dtb_ref, og_ref_g,
              og_ref, sfin_ref, s_ref, *, nb, nh, dk, dv):
    j = pl.program_id(1)
    last = pl.num_programs(1) - 1
    n = CHUNK
    bb = lambda m: m.astype(BF16)

    @pl.when(j == 0)
    def _():
        s_ref[...] = s0_ref[...]

    rows = lax.broadcasted_iota(jnp.int32, (n, LANES), 0)
    ri = lax.broadcasted_iota(jnp.int32, (n, n), 0)
    ci = lax.broadcasted_iota(jnp.int32, (n, n), 1)
    incl = ri >= ci
    strict = ri > ci

    qs, ks, vs, bcols, gcols, gls, decays = [], [], [], [], [], [], []
    for s in range(nb):
        xc = qkv_ref[s].astype(F32)
        ab = ab_ref[s]
        beta = _sigmoid(ab)
        a_in = pltpu.roll(ab, LANES - nh, 1) + dtb_ref[...]
        softplus = jnp.maximum(a_in, 0.0) + jnp.log(1.0 + jnp.exp(-jnp.abs(a_in)))
        gc = -jnp.exp(alog_ref[...]) * softplus
        sh = 1
        while sh < n:
            gc = gc + jnp.where(rows >= sh, pltpu.roll(gc, sh, 0), 0.0)
            sh *= 2
        gct = gc.T
        for h in range(nh):
            q = xc[:, h * dk:(h + 1) * dk]
            k = xc[:, nh * dk + h * dk:nh * dk + (h + 1) * dk]
            qs.append(q * lax.rsqrt(jnp.sum(q * q, axis=-1, keepdims=True) + EPS) * (dk ** -0.5))
            ks.append(k * lax.rsqrt(jnp.sum(k * k, axis=-1, keepdims=True) + EPS))
            vs.append(xc[:, 2 * nh * dk + h * dv:2 * nh * dk + (h + 1) * dv])
            bcols.append(beta[:, h:h + 1])
            gcol = gc[:, h:h + 1]
            gcols.append(gcol)
            gls.append(gc[n - 1:n, h:h + 1])
            decays.append(jnp.where(incl, jnp.exp(jnp.where(incl, gcol - gct[h:h + 1, :], 0.0)), 0.0))

    np_ = nb * nh
    qk_kk = [lax.dot_general(bb(jnp.concatenate([qs[p], ks[p]], axis=0)), bb(ks[p]),
                             (((1,), (1,)), ((), ())), preferred_element_type=F32) for p in range(np_)]
    attn = [bb(qk_kk[p][:n] * decays[p]) for p in range(np_)]
    ts = _unit_lower_inverse([jnp.where(strict, bcols[p] * qk_kk[p][n:] * decays[p], 0.0) for p in range(np_)])
    egs = [jnp.exp(g) for g in gcols]
    uw = [_dot(bb(ts[p]), bb(jnp.concatenate([bcols[p] * vs[p], (bcols[p] * egs[p]) * ks[p]], axis=1)))
          for p in range(np_)]
    kdt = [bb((ks[p] * jnp.exp(gls[p] - gcols[p])).T) for p in range(np_)]

    sidx = [(p // nh, p % nh) for p in range(np_)]
    s_old = [s_ref[a, h] for a, h in sidx]
    ws = [_dot(bb(jnp.concatenate([uw[p][:, dv:], qs[p] * egs[p]], axis=0)), bb(s_old[p])) for p in range(np_)]
    v_new = [bb(uw[p][:, :dv] - ws[p][:n]) for p in range(np_)]
    o_in = [_dot(attn[p], v_new[p]) for p in range(np_)]
    ds = [_dot(kdt[p], v_new[p]) for p in range(np_)]
    for p, (a, h) in enumerate(sidx):
        s_ref[a, h] = s_old[p] * jnp.exp(gls[p]) + ds[p]
        o = ws[p][n:] + o_in[p]
        on = o * lax.rsqrt(jnp.mean(o * o, axis=-1, keepdims=True) + EPS) * og_ref_g[...]
        zz = z_ref[a, :, h * dv:(h + 1) * dv].astype(F32)
        og_ref[a, :, h * dv:(h + 1) * dv] = (on * _silu(zz)).astype(BF16)

    @pl.when(j == last)
    def _():
        sfin_ref[...] = s_ref[...]


GDN_SEQS_PER_STEP = 4


def _gdn(qkvc, ab, z, s0, alog, dtb, onorm_g, *, nh, dk, dv):
    b, l, dq = qkvc.shape
    n = CHUNK
    nb = GDN_SEQS_PER_STEP if b % GDN_SEQS_PER_STEP == 0 else 1
    blk = lambda c: pl.BlockSpec((nb, n, c), lambda i, j: (i, j, 0))
    state = pl.BlockSpec((nb, nh, dk, dv), lambda i, j: (i, 0, 0, 0))
    return pl.pallas_call(
        functools.partial(_gdn_body, nb=nb, nh=nh, dk=dk, dv=dv),
        grid=(b // nb, l // n),
        in_specs=[blk(dq), blk(LANES), blk(nh * dv), state,
                  _resident(alog.shape), _resident(dtb.shape), _resident(onorm_g.shape)],
        out_specs=(blk(nh * dv), state),
        out_shape=(jax.ShapeDtypeStruct((b, l, nh * dv), BF16), jax.ShapeDtypeStruct((b, nh, dk, dv), F32)),
        scratch_shapes=[pltpu.VMEM((nb, nh, dk, dv), F32)],
        compiler_params=_params(2),
        name="gdn",
    )(qkvc, ab, z, s0, alog, dtb, onorm_g)


def _post_body(x_ref, va_ref, og_ref, ga_ref, gb_ref, woa_ref, wob_ref, wo_ref, g2_ref, wr_ref, br_ref,
               x1_ref, t_ref, ri_ref, rw_ref, cnt_ref, base_ref, *, n_exp, n_grp, epg, sub):
    i = pl.program_id(0)
    tm, d = x_ref.shape

    @pl.when(i == 0)
    def _():
        base_ref[...] = jnp.zeros_like(base_ref)

    chunks = [slice(c * sub, (c + 1) * sub) for c in range(d // sub)]

    def merged(cs):
        y_a = _dot(va_ref[...], woa_ref[:, cs])
        y_b = _dot(og_ref[...], wob_ref[:, cs])
        return (_sigmoid(ga_ref[:, cs].astype(F32)) * y_a + _sigmoid(gb_ref[:, cs].astype(F32)) * y_b).astype(BF16)

    mix = None
    pending = None
    for cs in chunks + [None]:
        m = merged(cs) if cs is not None else None
        if pending is not None:
            part = _dot(pending[1], wo_ref[pending[0], :])
            mix = part if mix is None else mix + part
        pending = (cs, m)
    x1 = x_ref[...] + mix
    x1_ref[...] = x1
    r = lax.rsqrt(jnp.mean(x1 * x1, axis=-1, keepdims=True) + EPS)
    t = x1 * r * g2_ref[...]
    _store_row_tiles(t_ref, t)
    logits = _dot(t.astype(BF16), wr_ref[...]) + br_ref[...]
    lane = lax.broadcasted_iota(jnp.int32, (tm, LANES), 1)
    lane_f = lane.astype(F32)
    neg = -jnp.inf

    def first_max(v):
        mx = jnp.max(v, axis=-1, keepdims=True)
        idx = jnp.min(jnp.where(v == mx, lane_f, float(LANES)), axis=-1, keepdims=True)
        return mx, idx.astype(jnp.int32)

    gmask = (lane >= n_exp) & (lane < n_exp + n_grp)
    gmax, gidx = first_max(jnp.where(gmask, logits, neg))
    gsel = gidx - n_exp
    pg_sel = 1.0 / jnp.sum(jnp.where(gmask, jnp.exp(logits - gmax), 0.0), axis=-1, keepdims=True)
    emask = (lane >= gsel * epg) & (lane < (gsel + 1) * epg)
    le = jnp.where(emask, logits, neg)
    m1, e1 = first_max(le)
    m2, e2 = first_max(jnp.where(lane == e1, neg, le))
    ex = jnp.exp(m2 - m1)
    p1 = 1.0 / (1.0 + ex)
    p2 = ex * p1
    onehot = ((lane == e1) | (lane == e2)).astype(BF16)
    rr = lax.broadcasted_iota(jnp.int32, (tm, tm), 0)
    cc = lax.broadcasted_iota(jnp.int32, (tm, tm), 1)
    before = _dot((rr > cc).astype(BF16), onehot) + base_ref[0:1, :]
    rank1 = jnp.sum(jnp.where(lane == e1, before, 0.0), axis=-1, keepdims=True).astype(jnp.int32)
    rank2 = jnp.sum(jnp.where(lane == e2, before, 0.0), axis=-1, keepdims=True).astype(jnp.int32)
    base_ref[0:1, :] = base_ref[0:1, :] + jnp.sum(onehot.astype(F32), axis=0, keepdims=True)
    ri_ref[...] = jnp.where(lane == 0, e1, jnp.where(lane == 1, e2, jnp.where(lane == 2, rank1, jnp.where(lane == 3, rank2, 0))))
    rw_ref[...] = jnp.where(lane == 0, pg_sel * p1, jnp.where(lane == 1, pg_sel * p2, 0.0))
    cnt_ref[...] = base_ref[...]


def _post(x, va, og, ga, gb, w_out_a, w_out_b, w_o, g2, w_r, b_r, *, n_exp, n_grp):
    t, d = x.shape
    tm = min(TOKEN_TILE, t)
    row = lambda n: pl.BlockSpec((tm, n), lambda i: (i, 0))
    return pl.pallas_call(
        functools.partial(_post_body, n_exp=n_exp, n_grp=n_grp, epg=n_exp // n_grp, sub=SUB_TILE),
        grid=(t // tm,),
        in_specs=[row(d), row(va.shape[1]), row(og.shape[1]), row(d), row(d),
                  _resident(w_out_a.shape), _resident(w_out_b.shape), _resident(w_o.shape),
                  _resident(g2.shape), _resident(w_r.shape), _resident(b_r.shape)],
        out_specs=(row(d), pl.BlockSpec((tm * SUBLANES, LANES), lambda i: (i, 0)), row(LANES), row(LANES),
                   pl.BlockSpec((SUBLANES, LANES), lambda i: (0, 0))),
        out_shape=(jax.ShapeDtypeStruct((t, d), F32), jax.ShapeDtypeStruct((t * SUBLANES, LANES), F32),
                   jax.ShapeDtypeStruct((t, LANES), jnp.int32), jax.ShapeDtypeStruct((t, LANES), F32),
                   jax.ShapeDtypeStruct((SUBLANES, LANES), F32)),
        scratch_shapes=[pltpu.VMEM((SUBLANES, LANES), F32)],
        compiler_params=_params(1),
        name="post",
    )(x, va, og, ga, gb, w_out_a, w_out_b, w_o, g2, w_r, b_r)


ROW_UNROLL = 8
TILE_ROWS = SUBLANES


def _scatter_body(ztile_ref, dest_ref, t_ref, xs_ref, zero_ref, buf_ref, sem, lsem, zsem, *, ts, r, n_zero):
    i = pl.program_id(0)
    n = pl.num_programs(0)
    rows = ts * TILE_ROWS

    def zero_copy(e):
        start = pl.multiple_of(ztile_ref[e] * TILE_ROWS, r * TILE_ROWS)
        return pltpu.make_async_copy(zero_ref, xs_ref.at[pl.ds(start, r * TILE_ROWS)], zsem)

    def load(step, slot):
        return pltpu.make_async_copy(t_ref.at[pl.ds(pl.multiple_of(step * rows, rows), rows)], buf_ref.at[slot], lsem.at[slot])

    def wait_rows(slot):
        for _ in range(2):
            pltpu.make_async_copy(buf_ref.at[slot], xs_ref.at[pl.ds(0, rows)], sem.at[slot]).wait()

    @pl.when(i == 0)
    def _():
        load(0, 0).start()
        zero_ref[...] = jnp.zeros_like(zero_ref)
        for e in range(n_zero):
            @pl.when(ztile_ref[e] >= 0)
            def _():
                zero_copy(e).start()
        for e in range(n_zero):
            @pl.when(ztile_ref[e] >= 0)
            def _():
                zero_copy(e).wait()

    slot = i % 2
    load(i, slot).wait()

    def start(q, c):
        src = buf_ref.at[slot, pl.ds(pl.multiple_of(q * TILE_ROWS, TILE_ROWS), TILE_ROWS)]
        for k in range(2):
            dst = pl.multiple_of(dest_ref[0, 0, 2 * q + k] * TILE_ROWS, TILE_ROWS)
            pltpu.make_async_copy(src, xs_ref.at[pl.ds(dst, TILE_ROWS)], sem.at[slot]).start(priority=k)
        return c

    lax.fori_loop(0, ts, start, 0, unroll=ROW_UNROLL)

    @pl.when(i > 0)
    def _():
        wait_rows(1 - slot)

    @pl.when(i + 1 < n)
    def _():
        load(i + 1, 1 - slot).start()

    @pl.when(i == n - 1)
    def _():
        wait_rows(slot)


def _scatter(zero_tile_start, dest, t_tiles, n_rows, r):
    n_tok = t_tiles.shape[0] // TILE_ROWS
    ts = min(SCATTER_TILE, n_tok)
    dest3 = dest.reshape(n_tok // ts, 1, 2 * ts)
    grid_spec = pltpu.PrefetchScalarGridSpec(
        num_scalar_prefetch=1,
        grid=(n_tok // ts,),
        in_specs=[pl.BlockSpec((1, 1, 2 * ts), lambda i, zt: (i, 0, 0), memory_space=pltpu.SMEM),
                  pl.BlockSpec(memory_space=pl.ANY)],
        out_specs=pl.BlockSpec(memory_space=pl.ANY),
        scratch_shapes=[pltpu.VMEM((r * TILE_ROWS, LANES), F32), pltpu.VMEM((2, ts * TILE_ROWS, LANES), F32),
                        pltpu.SemaphoreType.DMA((2,)), pltpu.SemaphoreType.DMA((2,)), pltpu.SemaphoreType.DMA(())],
    )
    return pl.pallas_call(
        functools.partial(_scatter_body, ts=ts, r=r, n_zero=zero_tile_start.shape[0]),
        grid_spec=grid_spec,
        out_shape=jax.ShapeDtypeStruct((n_rows * TILE_ROWS, LANES), F32),
        compiler_params=_params(1),
        name="moe_scatter",
    )(zero_tile_start, dest3, t_tiles)


def _experts_body(te_ref, nt_ref, xs_ref, wg_ref, wu_ref, wd_ref, ys_ref, *, r, sub):
    i = pl.program_id(0)

    @pl.when(i < nt_ref[0])
    def _():
        wg = wg_ref[...].astype(BF16)
        wu = wu_ref[...].astype(BF16)
        wd = wd_ref[...].astype(BF16)
        views = [pl.ds(s * sub * TILE_ROWS, sub * TILE_ROWS) for s in range(r // sub)]
        xs = [_load_row_tiles(xs_ref.at[v], sub).astype(BF16) for v in views]
        hg = [_dot(x, wg) for x in xs]
        hu = [_dot(x, wu) for x in xs]
        hdn = [(_silu(g) * u).astype(BF16) for g, u in zip(hg, hu)]
        ys = [_dot(h, wd) for h in hdn]
        for v, y in zip(views, ys):
            _store_row_tiles(ys_ref.at[v], y)

    @pl.when(i >= nt_ref[0])
    def _():
        ys_ref[...] = jnp.zeros_like(ys_ref)


def _experts(tile_expert, n_tiles_used, xs_tiles, wg, wu, wd, r):
    rows = xs_tiles.shape[0] // TILE_ROWS
    d, de = wg.shape[1], wg.shape[2]
    used = lambda i, nt: jnp.minimum(i, nt[0] - 1)
    grid_spec = pltpu.PrefetchScalarGridSpec(
        num_scalar_prefetch=2,
        grid=(rows // r,),
        in_specs=[pl.BlockSpec((r * TILE_ROWS, LANES), lambda i, te, nt: (used(i, nt), 0)),
                  pl.BlockSpec((None, d, de), lambda i, te, nt: (te[i], 0, 0)),
                  pl.BlockSpec((None, d, de), lambda i, te, nt: (te[i], 0, 0)),
                  pl.BlockSpec((None, de, d), lambda i, te, nt: (te[i], 0, 0))],
        out_specs=pl.BlockSpec((r * TILE_ROWS, LANES), lambda i, te, nt: (i, 0)),
    )
    return pl.pallas_call(
        functools.partial(_experts_body, r=r, sub=r),
        grid_spec=grid_spec,
        out_shape=jax.ShapeDtypeStruct(xs_tiles.shape, F32),
        compiler_params=_params(1),
        name="moe_experts",
    )(tile_expert, n_tiles_used, xs_tiles, wg, wu, wd)


def _combine_body(dnext_ref, dfirst_ref, x1_ref, rw_ref, ys_ref, gf_ref, y_ref, buf_ref, sem, *, ts):
    i = pl.program_id(0)
    n = pl.num_programs(0)

    def gather(dref, slot):
        def start(q, c):
            for k in range(2):
                src = pl.multiple_of(dref[0, 0, 2 * q + k] * TILE_ROWS, TILE_ROWS)
                dst = buf_ref.at[slot, k, pl.ds(pl.multiple_of(q * TILE_ROWS, TILE_ROWS), TILE_ROWS)]
                pltpu.make_async_copy(ys_ref.at[pl.ds(src, TILE_ROWS)], dst, sem.at[slot]).start(priority=k)
            return c
        lax.fori_loop(0, ts, start, 0, unroll=ROW_UNROLL)

    @pl.when(i == 0)
    def _():
        gather(dfirst_ref, 0)

    @pl.when(i + 1 < n)
    def _():
        gather(dnext_ref, (i + 1) % 2)

    slot = i % 2
    for k in range(2):
        pltpu.make_async_copy(ys_ref.at[pl.ds(0, ts * TILE_ROWS)], buf_ref.at[slot, k], sem.at[slot]).wait()
    rw = rw_ref[...]
    acc = rw[:, 0:1] * _load_row_tiles(buf_ref.at[slot, 0], ts) + rw[:, 1:2] * _load_row_tiles(buf_ref.at[slot, 1], ts)
    x2 = x1_ref[...] + acc
    r = lax.rsqrt(jnp.mean(x2 * x2, axis=-1, keepdims=True) + EPS)
    y_ref[...] = x2 * r * gf_ref[...]


def _combine(dest, x1, rw, ys_tiles, gf):
    n_tok, d = x1.shape
    ts = min(GATHER_TILE, n_tok)
    n = n_tok // ts
    dest3 = dest.reshape(n, 1, 2 * ts)
    return pl.pallas_call(
        functools.partial(_combine_body, ts=ts),
        grid=(n,),
        in_specs=[pl.BlockSpec((1, 1, 2 * ts), lambda i: (jnp.minimum(i + 1, n - 1), 0, 0), memory_space=pltpu.SMEM),
                  pl.BlockSpec((1, 1, 2 * ts), lambda i: (0, 0, 0), memory_space=pltpu.SMEM),
                  pl.BlockSpec((ts, d), lambda i: (i, 0)),
                  pl.BlockSpec((ts, LANES), lambda i: (i, 0)),
                  pl.BlockSpec(memory_space=pl.ANY),
                  _resident(gf.shape)],
        out_specs=pl.BlockSpec((ts, d), lambda i: (i, 0)),
        out_shape=jax.ShapeDtypeStruct((n_tok, d), F32),
        scratch_shapes=[pltpu.VMEM((2, 2, ts * TILE_ROWS, LANES), F32), pltpu.SemaphoreType.DMA((2,))],
        compiler_params=_params(1),
        name="moe_combine",
    )(dest3, dest3, x1, rw, ys_tiles, gf)


def _expert_tile(n_tok, n_exp):
    r = EXPERT_TILE
    while r > LANES and r > (2 * n_tok) // n_exp:
        r //= 2
    return r


def _moe(x1, t, ri, rw, counts, wg, wu, wd, gf, *, n_exp):
    n_tok, d = x1.shape
    r = _expert_tile(n_tok, n_exp)
    cnt = counts[0, :n_exp].astype(jnp.int32)
    padded = ((cnt + r - 1) // r) * r
    ends = jnp.cumsum(padded)
    offs = ends - padded
    n_tiles = (2 * n_tok) // r + n_exp
    tile_start = jnp.arange(n_tiles, dtype=jnp.int32) * r
    tile_expert = jnp.minimum(jnp.sum(tile_start[:, None] >= ends[None, :], axis=1), n_exp - 1).astype(jnp.int32)
    n_used = (ends[-1] // r).astype(jnp.int32).reshape(1)
    tail = ends[-1] + jnp.arange(n_exp, dtype=jnp.int32) * r
    zero_tile_start = jnp.concatenate([jnp.where(cnt > 0, ends - r, -1),
                                       jnp.where(tail < n_tiles * r, tail, -1)]).astype(jnp.int32)
    seg_start = jnp.sum(jnp.where(ri[:, 0:2, None] == jnp.arange(n_exp, dtype=jnp.int32), offs, 0), axis=-1)
    dest = seg_start + ri[:, 2:4]
    assert d == SUBLANES * LANES, "row-tile layout holds one 1024-wide f32 row per (8, 128) tile"
    xs = _scatter(zero_tile_start, dest, t, n_tiles * r, r)
    ys = _experts(tile_expert, n_used, xs, wg, wu, wd, r)
    return _combine(dest, x1, rw, ys, gf)


def _pad_rows8(c):
    return jnp.pad(c, ((0, 0), (SUBLANES - c.shape[1], 0), (0, 0)))


def _lane_row(v):
    return jnp.pad(v.astype(F32), (0, LANES - v.shape[0])).reshape(1, LANES)


def _run(x, prev_a, prev_qkv, s0, p, dims):
    b, l, d = x.shape
    dc, dq, dv, nh, dk, dvh, n_exp, n_grp = dims
    xf = x.reshape(b * l, d)
    va, qkvc, z, ga, gb, ab, tail_a, tail_q = _inproj(
        xf, l, p["g1"], p["w_main"], p["w_ab"], _pad_rows8(prev_a), _pad_rows8(prev_qkv),
        p["conv_a_w"], p["conv_qkv_w"], dc=dc, dq=dq, dv=dv, d=d)
    r3 = lambda a: a.reshape(b, l, a.shape[-1])
    og, s_fin = _gdn(r3(qkvc), r3(ab), r3(z), s0, p["alog"], p["dtb"], p["onorm_g"], nh=nh, dk=dk, dv=dvh)
    x1, t, ri, rw, counts = _post(xf, va, og.reshape(b * l, dv), ga, gb, p["w_out_a"], p["w_out_b"], p["w_o"],
                                  p["g2"], p["w_r"], p["b_r"], n_exp=n_exp, n_grp=n_grp)
    y = _moe(x1, t, ri, rw, counts, p["wg"], p["wu"], p["wd"], p["gf"], n_exp=n_exp)
    last = lambda tl, w: tl.reshape(b, -1, SUBLANES, tl.shape[-1])[:, -1, SUBLANES - w:, :]
    return y.reshape(b, l, d), last(tail_a, prev_a.shape[1]), last(tail_q, prev_qkv.shape[1]), s_fin


def kernel(x_prompt, x_sample, cache_conv_a, cache_conv_qkv, state_gdn, norm1_g, w_in, conv_a_w, w_out_a, conv_qkv_w, a_log, dt_bias, onorm_g, w_out_b, w_o, norm2_g, w_router_group, b_router_group, w_router_expert, b_router_expert, w_gate, w_up, w_down, final_g):
    depth = w_in.shape[0]
    assert depth == 1, "single trunk layer"
    d = x_prompt.shape[-1]
    dc = conv_a_w.shape[-1]
    dq = conv_qkv_w.shape[-1]
    nh = a_log.shape[-1]
    dk, dvh = state_gdn.shape[-2], state_gdn.shape[-1]
    dv = nh * dvh
    n_grp = w_router_group.shape[-1]
    n_exp = w_router_expert.shape[-1]
    dims = (dc, dq, dv, nh, dk, dvh, n_exp, n_grp)

    w = w_in[0]
    o = [0, dc, 2 * dc, 3 * dc, 3 * dc + dq, 3 * dc + dq + dv, 3 * dc + dq + dv + nh, 3 * dc + dq + dv + 2 * nh,
         3 * dc + dq + dv + 2 * nh + d, 3 * dc + dq + dv + 2 * nh + 2 * d]
    seg = lambda k: w[:, o[k]:o[k + 1]]
    w_main = jnp.concatenate([seg(0), seg(2), seg(1), seg(3), seg(4), seg(7), seg(8)], axis=1).astype(BF16)
    w_ab = jnp.pad(jnp.concatenate([seg(5), seg(6)], axis=1), ((0, 0), (0, LANES - 2 * nh))).astype(BF16)
    w_r = jnp.pad(jnp.concatenate([w_router_expert[0], w_router_group[0]], axis=1),
                  ((0, 0), (0, LANES - n_exp - n_grp))).astype(BF16)
    b_r = _lane_row(jnp.concatenate([b_router_expert[0], b_router_group[0]]))
    p = dict(
        g1=norm1_g[0].reshape(1, d), w_main=w_main, w_ab=w_ab,
        conv_a_w=conv_a_w[0], w_out_a=w_out_a[0].astype(BF16),
        conv_qkv_w=conv_qkv_w[0], alog=_lane_row(a_log[0]), dtb=_lane_row(dt_bias[0]),
        onorm_g=onorm_g[0].reshape(1, dvh),
        w_out_b=w_out_b[0].astype(BF16), w_o=w_o[0].astype(BF16), g2=norm2_g[0].reshape(1, d),
        w_r=w_r, b_r=b_r,
        wg=w_gate[0], wu=w_up[0], wd=w_down[0],
        gf=final_g.reshape(1, d),
    )
    bp = x_prompt.shape[0]
    dt_ = x_prompt.dtype
    zero_a = jnp.zeros((bp,) + cache_conv_a.shape[2:], dt_)
    zero_qkv = jnp.zeros((bp,) + cache_conv_qkv.shape[2:], dt_)
    zero_s = jnp.zeros((bp,) + state_gdn.shape[2:], dt_)
    y_p, a_p, q_p, s_p = _run(x_prompt, zero_a, zero_qkv, zero_s, p, dims)
    y_s, a_s, q_s, s_s = _run(x_sample, cache_conv_a[0], cache_conv_qkv[0], state_gdn[0], p, dims)
    return (y_p, y_s, a_p[None], q_p[None], s_p[None], a_s[None], q_s[None], s_s[None])
```

```python
import functools

import jax
import jax.numpy as jnp
from jax import lax
from jax.experimental import pallas as pl
from jax.experimental.pallas import tpu as pltpu

F32 = jnp.float32
BF16 = jnp.bfloat16
EPS = 1e-6
CHUNK = 64
LANES = 128
SUBLANES = 8
VMEM_LIMIT_BYTES = 56 * 1024 * 1024
TOKEN_TILE = 512
SUB_TILE = 256
EXPERT_TILE = 512
SCATTER_TILE = 1024
GATHER_TILE = 256


def _params(n_axes):
    return pltpu.CompilerParams(dimension_semantics=("arbitrary",) * n_axes,
                                vmem_limit_bytes=VMEM_LIMIT_BYTES)


def _resident(shape):
    return pl.BlockSpec(shape, lambda *_: (0,) * len(shape), pipeline_mode=pl.Buffered(1))


def _dot(a, b):
    return jnp.dot(a, b, preferred_element_type=F32)


def _sigmoid(x):
    return 0.5 * jnp.tanh(0.5 * x) + 0.5


def _silu(x):
    h = 0.5 * x
    return h * (1.0 + jnp.tanh(h))


def _store_row_tiles(ref, x):
    n = x.shape[0]
    for c in range(SUBLANES):
        ref[pl.ds(c, n, stride=SUBLANES), :] = x[:, c * LANES:(c + 1) * LANES]


def _load_row_tiles(ref, n):
    return jnp.concatenate([ref[pl.ds(c, n, stride=SUBLANES), :] for c in range(SUBLANES)], axis=1)


def _causal_conv(u, prev8, w_ref, col0):
    width = w_ref.shape[0]
    n, c = u.shape
    tap = lambda j: w_ref[j:j + 1, col0:col0 + c]
    u1 = pltpu.roll(u, 1, 0)
    body = None
    for m in range((width + 1) // 2):
        hi = width - 1 - 2 * m
        pair = tap(hi) * u
        if hi >= 1:
            pair = pair + tap(hi - 1) * u1
        body = pair if m == 0 else body + pltpu.roll(pair, 2 * m, 0)
    head_in = jnp.concatenate([prev8, u[0:SUBLANES]], axis=0)
    lo = SUBLANES - (width - 1)
    head = tap(0) * head_in[lo:lo + SUBLANES]
    for j in range(1, width):
        head = head + tap(j) * head_in[lo + j:lo + j + SUBLANES]
    if n == SUBLANES:
        return head
    return jnp.concatenate([head, body[SUBLANES:]], axis=0)


def _inproj_body(x_ref, g_ref, w_ref, wab_ref, ca_ref, cq_ref, cwa_ref, cwq_ref,
                 va_ref, qkvc_ref, z_ref, ga_ref, gb_ref, ab_ref, ta_ref, tq_ref,
                 carry_a, carry_q, *, dc, dq, seg, nseg, tiles_per_seq, cw):
    i = pl.program_id(0)
    starts_seq = (i % tiles_per_seq) == 0
    x = x_ref[...]
    r = lax.rsqrt(jnp.mean(x * x, axis=-1, keepdims=True) + EPS)
    xn = (x * r * g_ref[...]).astype(BF16)

    def mm(lo, n):
        return _dot(xn, w_ref[:, lo:lo + n])

    def conv(val, cache_ref, carry_ref, tail_ref, cw_ref, col0):
        c = val.shape[1]
        outs = []
        for s in range(nseg):
            v = val[s * seg:(s + 1) * seg]
            prev8 = jnp.where(starts_seq, cache_ref[s, :, col0:col0 + c], carry_ref[s, :, col0:col0 + c])
            outs.append(_causal_conv(v, prev8, cw_ref, col0))
            last8 = v[seg - SUBLANES:seg]
            carry_ref[s, :, col0:col0 + c] = last8
            tail_ref[s, :, col0:col0 + c] = last8
        return outs[0] if nseg == 1 else jnp.concatenate(outs, axis=0)

    def mixer_a_task(lo):
        def products():
            return mm(lo, cw), mm(dc + lo, cw), mm(2 * dc + lo, cw)

        def finish(h, c, b):
            va_ref[:, lo:lo + cw] = (b * conv(c * h, ca_ref, carry_a, ta_ref, cwa_ref, lo)).astype(BF16)
        return products, finish

    def qkv_task(lo):
        def products():
            return (mm(3 * dc + lo, cw),)

        def finish(raw):
            qc = conv(raw, cq_ref, carry_q, tq_ref, cwq_ref, lo)
            qkvc_ref[:, lo:lo + cw] = _silu(qc).astype(BF16)
        return products, finish

    def plain_task(ref, o, lo):
        def products():
            return (mm(o + lo, cw),)

        def finish(raw):
            ref[:, lo:lo + cw] = raw.astype(BF16)
        return products, finish

    heavy = [qkv_task(lo) for lo in range(0, dq, cw)] + [mixer_a_task(lo) for lo in range(0, dc, cw)]
    light = []
    o = 3 * dc + dq
    for ref in (z_ref, ga_ref, gb_ref):
        light += [plain_task(ref, o, lo) for lo in range(0, ref.shape[1], cw)]
        o += ref.shape[1]
    tasks = []
    for k in range(max(len(heavy), len(light))):
        tasks += heavy[k:k + 1] + light[k:k + 1]
    ready = tasks[0][0]()
    for k, (_, finish) in enumerate(tasks):
        nxt = tasks[k + 1][0]() if k + 1 < len(tasks) else None
        finish(*ready)
        ready = nxt
    ab_ref[...] = _dot(xn, wab_ref[...])


def _inproj(x, l, g1, w_main, w_ab, cache_a8, cache_q8, conv_a_w, conv_qkv_w, *, dc, dq, dv, d):
    t = x.shape[0]
    tm = min(TOKEN_TILE, t)
    seg = min(tm, l)
    nseg = tm // seg
    tiles_per_seq = l // seg
    n = t // tm
    row = lambda c: pl.BlockSpec((tm, c), lambda i: (i, 0))
    cache = lambda c: pl.BlockSpec((nseg, SUBLANES, c), lambda i: (i // tiles_per_seq, 0, 0))
    tail = lambda c: pl.BlockSpec((nseg, SUBLANES, c), lambda i: (i, 0, 0))
    out_shapes = (
        jax.ShapeDtypeStruct((t, dc), BF16), jax.ShapeDtypeStruct((t, dq), BF16), jax.ShapeDtypeStruct((t, dv), BF16),
        jax.ShapeDtypeStruct((t, d), BF16), jax.ShapeDtypeStruct((t, d), BF16), jax.ShapeDtypeStruct((t, LANES), F32),
        jax.ShapeDtypeStruct((n * nseg, SUBLANES, dc), F32), jax.ShapeDtypeStruct((n * nseg, SUBLANES, dq), F32))
    return pl.pallas_call(
        functools.partial(_inproj_body, dc=dc, dq=dq, seg=seg, nseg=nseg, tiles_per_seq=tiles_per_seq, cw=SUB_TILE),
        grid=(n,),
        in_specs=[row(d), _resident((1, d)), _resident(w_main.shape), _resident(w_ab.shape),
                  cache(dc), cache(dq), _resident(conv_a_w.shape), _resident(conv_qkv_w.shape)],
        out_specs=(row(dc), row(dq), row(dv), row(d), row(d), row(LANES), tail(dc), tail(dq)),
        out_shape=out_shapes,
        scratch_shapes=[pltpu.VMEM((nseg, SUBLANES, dc), F32), pltpu.VMEM((nseg, SUBLANES, dq), F32)],
        compiler_params=_params(1),
        name="inproj",
    )(x, g1, w_main, w_ab, cache_a8, cache_q8, conv_a_w, conv_qkv_w)


def _unit_lower_inverse(mats):
    n = mats[0].shape[0]
    r = lax.broadcasted_iota(jnp.int32, (n, n), 0)
    c = lax.broadcasted_iota(jnp.int32, (n, n), 1)
    eye = (r == c).astype(F32)

    def same_block(size):
        return jnp.bitwise_xor(r, c) < size

    bb = lambda m: m.astype(BF16)
    a8 = [jnp.where(same_block(SUBLANES), a, 0.0) for a in mats]
    p2 = [bb(_dot(bb(a), bb(a))) for a in a8]
    p4 = [bb(_dot(p, p)) for p in p2]
    xs = [eye - a for a in a8]
    xs = [x + _dot(bb(x), p) for x, p in zip(xs, p2)]
    xs = [x + _dot(bb(x), p) for x, p in zip(xs, p4)]
    size = SUBLANES
    while size < n:
        in_pair = same_block(2 * size) & jnp.logical_not(same_block(size))
        ys = [_dot(bb(jnp.where(in_pair, a, 0.0)), bb(x)) for a, x in zip(mats, xs)]
        xs = [x - _dot(bb(x), bb(y)) for x, y in zip(xs, ys)]
        size *= 2
    return xs


def _gdn_body(qkv_ref, ab_ref, z_ref, s0_ref, alog_ref, dtb_ref, og_ref_g,
              og_ref, sfin_ref, s_ref, *, nb, nh, dk, dv):
    j = pl.program_id(1)
    last = pl.num_programs(1) - 1
    n = CHUNK
    bb = lambda m: m.astype(BF16)

    @pl.when(j == 0)
    def _():
        s_ref[...] = s0_ref[...]

    rows = lax.broadcasted_iota(jnp.int32, (n, LANES), 0)
    ri = lax.broadcasted_iota(jnp.int32, (n, n), 0)
    ci = lax.broadcasted_iota(jnp.int32, (n, n), 1)
    incl = ri >= ci
    strict = ri > ci

    qs, ks, vs, bcols, gcols, gls, decays = [], [], [], [], [], [], []
    for s in range(nb):
        xc = qkv_ref[s].astype(F32)
        ab = ab_ref[s]
        beta = _sigmoid(ab)
        a_in = pltpu.roll(ab, LANES - nh, 1) + dtb_ref[...]
        softplus = jnp.maximum(a_in, 0.0) + jnp.log(1.0 + jnp.exp(-jnp.abs(a_in)))
        gc = -jnp.exp(alog_ref[...]) * softplus
        sh = 1
        while sh < n:
            gc = gc + jnp.where(rows >= sh, pltpu.roll(gc, sh, 0), 0.0)
            sh *= 2
        gct = gc.T
        for h in range(nh):
            q = xc[:, h * dk:(h + 1) * dk]
            k = xc[:, nh * dk + h * dk:nh * dk + (h + 1) * dk]
            qs.append(q * (lax.rsqrt(jnp.sum(q * q, axis=-1, keepdims=True) + EPS) * (dk ** -0.5)))
            ks.append(k * lax.rsqrt(jnp.sum(k * k, axis=-1, keepdims=True) + EPS))
            vs.append(xc[:, 2 * nh * dk + h * dv:2 * nh * dk + (h + 1) * dv])
            bcols.append(beta[:, h:h + 1])
            gcol = gc[:, h:h + 1]
            gcols.append(gcol)
            gls.append(gc[n - 1:n, h:h + 1])
            decays.append(jnp.where(incl, jnp.exp(jnp.where(incl, gcol - gct[h:h + 1, :], 0.0)), 0.0))

    np_ = nb * nh
    qk_kk = [lax.dot_general(bb(jnp.concatenate([qs[p], ks[p]], axis=0)), bb(ks[p]),
                             (((1,), (1,)), ((), ())), preferred_element_type=F32) for p in range(np_)]
    attn = [bb(qk_kk[p][:n] * decays[p]) for p in range(np_)]
    ts = _unit_lower_inverse([jnp.where(strict, bcols[p] * qk_kk[p][n:] * decays[p], 0.0) for p in range(np_)])
    egs = [jnp.exp(g) for g in gcols]
    uw = [_dot(bb(ts[p]), bb(jnp.concatenate([bcols[p] * vs[p], (bcols[p] * egs[p]) * ks[p]], axis=1)))
          for p in range(np_)]
    kdt = [bb((ks[p] * jnp.exp(gls[p] - gcols[p])).T) for p in range(np_)]

    sidx = [(p // nh, p % nh) for p in range(np_)]
    s_old = [s_ref[a, h] for a, h in sidx]
    ws = [_dot(bb(jnp.concatenate([uw[p][:, dv:], qs[p] * egs[p]], axis=0)), bb(s_old[p])) for p in range(np_)]
    v_new = [bb(uw[p][:, :dv] - ws[p][:n]) for p in range(np_)]
    o_in = [_dot(attn[p], v_new[p]) for p in range(np_)]
    ds = [_dot(kdt[p], v_new[p]) for p in range(np_)]
    for p, (a, h) in enumerate(sidx):
        s_ref[a, h] = s_old[p] * jnp.exp(gls[p]) + ds[p]
        o = ws[p][n:] + o_in[p]
        on = o * lax.rsqrt(jnp.mean(o * o, axis=-1, keepdims=True) + EPS) * og_ref_g[...]
        zz = z_ref[a, :, h * dv:(h + 1) * dv].astype(F32)
        og_ref[a, :, h * dv:(h + 1) * dv] = (on * _silu(zz)).astype(BF16)

    @pl.when(j == last)
    def _():
        sfin_ref[...] = s_ref[...]


GDN_SEQS_PER_STEP = 4


def _gdn(qkvc, ab, z, s0, alog, dtb, onorm_g, *, nh, dk, dv):
    b, l, dq = qkvc.shape
    n = CHUNK
    nb = GDN_SEQS_PER_STEP if b % GDN_SEQS_PER_STEP == 0 else 1
    blk = lambda c: pl.BlockSpec((nb, n, c), lambda i, j: (i, j, 0))
    state = pl.BlockSpec((nb, nh, dk, dv), lambda i, j: (i, 0, 0, 0))
    return pl.pallas_call(
        functools.partial(_gdn_body, nb=nb, nh=nh, dk=dk, dv=dv),
        grid=(b // nb, l // n),
        in_specs=[blk(dq), blk(LANES), blk(nh * dv), state,
                  _resident(alog.shape), _resident(dtb.shape), _resident(onorm_g.shape)],
        out_specs=(blk(nh * dv), state),
        out_shape=(jax.ShapeDtypeStruct((b, l, nh * dv), BF16), jax.ShapeDtypeStruct((b, nh, dk, dv), F32)),
        scratch_shapes=[pltpu.VMEM((nb, nh, dk, dv), F32)],
        compiler_params=_params(2),
        name="gdn",
    )(qkvc, ab, z, s0, alog, dtb, onorm_g)


def _post_body(x_ref, va_ref, og_ref, ga_ref, gb_ref, woa_ref, wob_ref, wo_ref, g2_ref, wr_ref, br_ref,
               x1_ref, t_ref, ri_ref, rw_ref, cnt_ref, base_ref, *, n_exp, n_grp, epg, sub):
    i = pl.program_id(0)
    tm, d = x_ref.shape

    @pl.when(i == 0)
    def _():
        base_ref[...] = jnp.zeros_like(base_ref)

    chunks = [slice(c * sub, (c + 1) * sub) for c in range(d // sub)]

    def merged(cs):
        y_a = _dot(va_ref[...], woa_ref[:, cs])
        y_b = _dot(og_ref[...], wob_ref[:, cs])
        return (_sigmoid(ga_ref[:, cs].astype(F32)) * y_a + _sigmoid(gb_ref[:, cs].astype(F32)) * y_b).astype(BF16)

    mix = None
    pending = None
    for cs in chunks + [None]:
        m = merged(cs) if cs is not None else None
        if pending is not None:
            part = _dot(pending[1], wo_ref[pending[0], :])
            mix = part if mix is None else mix + part
        pending = (cs, m)
    x1 = x_ref[...] + mix
    x1_ref[...] = x1
    r = lax.rsqrt(jnp.mean(x1 * x1, axis=-1, keepdims=True) + EPS)
    t = x1 * r * g2_ref[...]
    _store_row_tiles(t_ref, t)
    logits = _dot(t.astype(BF16), wr_ref[...]) + br_ref[...]
    lane = lax.broadcasted_iota(jnp.int32, (tm, LANES), 1)
    lane_f = lane.astype(F32)
    neg = -jnp.inf

    def first_max(v):
        mx = jnp.max(v, axis=-1, keepdims=True)
        idx = jnp.min(jnp.where(v == mx, lane_f, float(LANES)), axis=-1, keepdims=True)
        return mx, idx.astype(jnp.int32)

    gmask = (lane >= n_exp) & (lane < n_exp + n_grp)
    gmax, gidx = first_max(jnp.where(gmask, logits, neg))
    gsel = gidx - n_exp
    pg_sel = 1.0 / jnp.sum(jnp.where(gmask, jnp.exp(logits - gmax), 0.0), axis=-1, keepdims=True)
    emask = (lane >= gsel * epg) & (lane < (gsel + 1) * epg)
    le = jnp.where(emask, logits, neg)
    m1, e1 = first_max(le)
    m2, e2 = first_max(jnp.where(lane == e1, neg, le))
    ex = jnp.exp(m2 - m1)
    p1 = 1.0 / (1.0 + ex)
    p2 = ex * p1
    onehot = ((lane == e1) | (lane == e2)).astype(BF16)
    rr = lax.broadcasted_iota(jnp.int32, (tm, tm), 0)
    cc = lax.broadcasted_iota(jnp.int32, (tm, tm), 1)
    before = _dot((rr > cc).astype(BF16), onehot) + base_ref[0:1, :]
    rank1 = jnp.sum(jnp.where(lane == e1, before, 0.0), axis=-1, keepdims=True).astype(jnp.int32)
    rank2 = jnp.sum(jnp.where(lane == e2, before, 0.0), axis=-1, keepdims=True).astype(jnp.int32)
    base_ref[0:1, :] = base_ref[0:1, :] + jnp.sum(onehot.astype(F32), axis=0, keepdims=True)
    ri = jnp.where(lane == 0, e1, jnp.where(lane == 1, e2, jnp.where(lane == 2, rank1, jnp.where(lane == 3, rank2, 0))))
    ri_ref[...] = ri.T[0:SUBLANES, :]
    rw_ref[...] = jnp.where(lane == 0, pg_sel * p1, jnp.where(lane == 1, pg_sel * p2, 0.0))
    cnt_ref[...] = base_ref[...]


def _post(x, va, og, ga, gb, w_out_a, w_out_b, w_o, g2, w_r, b_r, *, n_exp, n_grp):
    t, d = x.shape
    tm = min(TOKEN_TILE, t)
    row = lambda n: pl.BlockSpec((tm, n), lambda i: (i, 0))
    return pl.pallas_call(
        functools.partial(_post_body, n_exp=n_exp, n_grp=n_grp, epg=n_exp // n_grp, sub=SUB_TILE),
        grid=(t // tm,),
        in_specs=[row(d), row(va.shape[1]), row(og.shape[1]), row(d), row(d),
                  _resident(w_out_a.shape), _resident(w_out_b.shape), _resident(w_o.shape),
                  _resident(g2.shape), _resident(w_r.shape), _resident(b_r.shape)],
        out_specs=(row(d), pl.BlockSpec((tm * SUBLANES, LANES), lambda i: (i, 0)),
                   pl.BlockSpec((SUBLANES, tm), lambda i: (0, i)), row(LANES),
                   pl.BlockSpec((SUBLANES, LANES), lambda i: (0, 0))),
        out_shape=(jax.ShapeDtypeStruct((t, d), F32), jax.ShapeDtypeStruct((t * SUBLANES, LANES), F32),
                   jax.ShapeDtypeStruct((SUBLANES, t), jnp.int32), jax.ShapeDtypeStruct((t, LANES), F32),
                   jax.ShapeDtypeStruct((SUBLANES, LANES), F32)),
        scratch_shapes=[pltpu.VMEM((SUBLANES, LANES), F32)],
        compiler_params=_params(1),
        name="post",
    )(x, va, og, ga, gb, w_out_a, w_out_b, w_o, g2, w_r, b_r)


ROW_UNROLL = 8
TILE_ROWS = SUBLANES


def _scatter_body(ztile_ref, dest_ref, t_ref, xs_ref, zero_ref, buf_ref, sem, lsem, zsem, *, ts, r, n_zero):
    i = pl.program_id(0)
    n = pl.num_programs(0)
    rows = ts * TILE_ROWS

    def zero_copy(e):
        start = pl.multiple_of(ztile_ref[e] * TILE_ROWS, r * TILE_ROWS)
        return pltpu.make_async_copy(zero_ref, xs_ref.at[pl.ds(start, r * TILE_ROWS)], zsem)

    def load(step, slot):
        return pltpu.make_async_copy(t_ref.at[pl.ds(pl.multiple_of(step * rows, rows), rows)], buf_ref.at[slot], lsem.at[slot])

    def wait_rows(slot):
        for _ in range(2):
            pltpu.make_async_copy(buf_ref.at[slot], xs_ref.at[pl.ds(0, rows)], sem.at[slot]).wait()

    @pl.when(i == 0)
    def _():
        load(0, 0).start()
        zero_ref[...] = jnp.zeros_like(zero_ref)
        for e in range(n_zero):
            @pl.when(ztile_ref[e] >= 0)
            def _():
                zero_copy(e).start()
        for e in range(n_zero):
            @pl.when(ztile_ref[e] >= 0)
            def _():
                zero_copy(e).wait()

    slot = i % 2
    load(i, slot).wait()

    def start(q, c):
        src = buf_ref.at[slot, pl.ds(pl.multiple_of(q * TILE_ROWS, TILE_ROWS), TILE_ROWS)]
        for k in range(2):
            dst = pl.multiple_of(dest_ref[0, 0, k * ts + q] * TILE_ROWS, TILE_ROWS)
            pltpu.make_async_copy(src, xs_ref.at[pl.ds(dst, TILE_ROWS)], sem.at[slot]).start(priority=k)
        return c

    lax.fori_loop(0, ts, start, 0, unroll=ROW_UNROLL)

    @pl.when(i > 0)
    def _():
        wait_rows(1 - slot)

    @pl.when(i + 1 < n)
    def _():
        load(i + 1, 1 - slot).start()

    @pl.when(i == n - 1)
    def _():
        wait_rows(slot)


def _scatter(zero_tile_start, dest, t_tiles, n_rows, r):
    n_tok = t_tiles.shape[0] // TILE_ROWS
    ts = min(SCATTER_TILE, n_tok)
    dest3 = _per_step(dest, ts)
    grid_spec = pltpu.PrefetchScalarGridSpec(
        num_scalar_prefetch=1,
        grid=(n_tok // ts,),
        in_specs=[pl.BlockSpec((1, 1, 2 * ts), lambda i, zt: (i, 0, 0), memory_space=pltpu.SMEM),
                  pl.BlockSpec(memory_space=pl.ANY)],
        out_specs=pl.BlockSpec(memory_space=pl.ANY),
        scratch_shapes=[pltpu.VMEM((r * TILE_ROWS, LANES), F32), pltpu.VMEM((2, ts * TILE_ROWS, LANES), F32),
                        pltpu.SemaphoreType.DMA((2,)), pltpu.SemaphoreType.DMA((2,)), pltpu.SemaphoreType.DMA(())],
    )
    return pl.pallas_call(
        functools.partial(_scatter_body, ts=ts, r=r, n_zero=zero_tile_start.shape[0]),
        grid_spec=grid_spec,
        out_shape=jax.ShapeDtypeStruct((n_rows * TILE_ROWS, LANES), F32),
        compiler_params=_params(1),
        name="moe_scatter",
    )(zero_tile_start, dest3, t_tiles)


def _experts_body(te_ref, nt_ref, xs_ref, wg_ref, wu_ref, wd_ref, ys_ref, *, r, sub):
    i = pl.program_id(0)

    @pl.when(i < nt_ref[0])
    def _():
        wg = wg_ref[...].astype(BF16)
        wu = wu_ref[...].astype(BF16)
        wd = wd_ref[...].astype(BF16)
        views = [pl.ds(s * sub * TILE_ROWS, sub * TILE_ROWS) for s in range(r // sub)]
        xs = [_load_row_tiles(xs_ref.at[v], sub).astype(BF16) for v in views]
        hg = [_dot(x, wg) for x in xs]
        hu = [_dot(x, wu) for x in xs]
        hdn = [(_silu(g) * u).astype(BF16) for g, u in zip(hg, hu)]
        ys = [_dot(h, wd) for h in hdn]
        for v, y in zip(views, ys):
            _store_row_tiles(ys_ref.at[v], y)

    @pl.when(i >= nt_ref[0])
    def _():
        ys_ref[...] = jnp.zeros_like(ys_ref)


def _experts(tile_expert, n_tiles_used, xs_tiles, wg, wu, wd, r):
    rows = xs_tiles.shape[0] // TILE_ROWS
    d, de = wg.shape[1], wg.shape[2]
    used = lambda i, nt: jnp.minimum(i, nt[0] - 1)
    grid_spec = pltpu.PrefetchScalarGridSpec(
        num_scalar_prefetch=2,
        grid=(rows // r,),
        in_specs=[pl.BlockSpec((r * TILE_ROWS, LANES), lambda i, te, nt: (used(i, nt), 0)),
                  pl.BlockSpec((None, d, de), lambda i, te, nt: (te[i], 0, 0)),
                  pl.BlockSpec((None, d, de), lambda i, te, nt: (te[i], 0, 0)),
                  pl.BlockSpec((None, de, d), lambda i, te, nt: (te[i], 0, 0))],
        out_specs=pl.BlockSpec((r * TILE_ROWS, LANES), lambda i, te, nt: (i, 0)),
    )
    return pl.pallas_call(
        functools.partial(_experts_body, r=r, sub=r),
        grid_spec=grid_spec,
        out_shape=jax.ShapeDtypeStruct(xs_tiles.shape, F32),
        compiler_params=_params(1),
        name="moe_experts",
    )(tile_expert, n_tiles_used, xs_tiles, wg, wu, wd)


def _combine_body(dnext_ref, dfirst_ref, x1_ref, rw_ref, ys_ref, gf_ref, y_ref, buf_ref, sem, *, ts):
    i = pl.program_id(0)
    n = pl.num_programs(0)

    def gather(dref, slot):
        def start(q, c):
            for k in range(2):
                src = pl.multiple_of(dref[0, 0, k * ts + q] * TILE_ROWS, TILE_ROWS)
                dst = buf_ref.at[slot, k, pl.ds(pl.multiple_of(q * TILE_ROWS, TILE_ROWS), TILE_ROWS)]
                pltpu.make_async_copy(ys_ref.at[pl.ds(src, TILE_ROWS)], dst, sem.at[slot]).start(priority=k)
            return c
        lax.fori_loop(0, ts, start, 0, unroll=ROW_UNROLL)

    @pl.when(i == 0)
    def _():
        gather(dfirst_ref, 0)

    @pl.when(i + 1 < n)
    def _():
        gather(dnext_ref, (i + 1) % 2)

    slot = i % 2
    for k in range(2):
        pltpu.make_async_copy(ys_ref.at[pl.ds(0, ts * TILE_ROWS)], buf_ref.at[slot, k], sem.at[slot]).wait()
    rw = rw_ref[...]
    acc = rw[:, 0:1] * _load_row_tiles(buf_ref.at[slot, 0], ts) + rw[:, 1:2] * _load_row_tiles(buf_ref.at[slot, 1], ts)
    x2 = x1_ref[...] + acc
    r = lax.rsqrt(jnp.mean(x2 * x2, axis=-1, keepdims=True) + EPS)
    y_ref[...] = x2 * r * gf_ref[...]


def _combine(dest, x1, rw, ys_tiles, gf):
    n_tok, d = x1.shape
    ts = min(GATHER_TILE, n_tok)
    n = n_tok // ts
    dest3 = _per_step(dest, ts)
    return pl.pallas_call(
        functools.partial(_combine_body, ts=ts),
        grid=(n,),
        in_specs=[pl.BlockSpec((1, 1, 2 * ts), lambda i: (jnp.minimum(i + 1, n - 1), 0, 0), memory_space=pltpu.SMEM),
                  pl.BlockSpec((1, 1, 2 * ts), lambda i: (0, 0, 0), memory_space=pltpu.SMEM),
                  pl.BlockSpec((ts, d), lambda i: (i, 0)),
                  pl.BlockSpec((ts, LANES), lambda i: (i, 0)),
                  pl.BlockSpec(memory_space=pl.ANY),
                  _resident(gf.shape)],
        out_specs=pl.BlockSpec((ts, d), lambda i: (i, 0)),
        out_shape=jax.ShapeDtypeStruct((n_tok, d), F32),
        scratch_shapes=[pltpu.VMEM((2, 2, ts * TILE_ROWS, LANES), F32), pltpu.SemaphoreType.DMA((2,))],
        compiler_params=_params(1),
        name="moe_combine",
    )(dest3, dest3, x1, rw, ys_tiles, gf)


def _per_step(dest, ts):
    n = dest.shape[1] // ts
    return dest.reshape(2, n, ts).transpose(1, 0, 2).reshape(n, 1, 2 * ts)


def _expert_tile(n_tok, n_exp):
    r = EXPERT_TILE
    while r > LANES and r > (2 * n_tok) // n_exp:
        r //= 2
    return r


def _moe(x1, t, ri, rw, counts, wg, wu, wd, gf, *, n_exp):
    n_tok, d = x1.shape
    r = _expert_tile(n_tok, n_exp)
    cnt = counts[0, :n_exp].astype(jnp.int32)
    padded = ((cnt + r - 1) // r) * r
    ends = jnp.cumsum(padded)
    offs = ends - padded
    n_tiles = (2 * n_tok) // r + n_exp
    tile_start = jnp.arange(n_tiles, dtype=jnp.int32) * r
    tile_expert = jnp.minimum(jnp.sum(tile_start[:, None] >= ends[None, :], axis=1), n_exp - 1).astype(jnp.int32)
    n_used = (ends[-1] // r).astype(jnp.int32).reshape(1)
    tail = ends[-1] + jnp.arange(n_exp, dtype=jnp.int32) * r
    zero_tile_start = jnp.concatenate([jnp.where(cnt > 0, ends - r, -1),
                                       jnp.where(tail < n_tiles * r, tail, -1)]).astype(jnp.int32)
    experts = jnp.arange(n_exp, dtype=jnp.int32)[:, None, None]
    seg_start = jnp.sum(jnp.where(ri[None, 0:2, :] == experts, offs[:, None, None], 0), axis=0)
    dest = seg_start + ri[2:4, :]
    assert d == SUBLANES * LANES, "row-tile layout holds one 1024-wide f32 row per (8, 128) tile"
    xs = _scatter(zero_tile_start, dest, t, n_tiles * r, r)
    ys = _experts(tile_expert, n_used, xs, wg, wu, wd, r)
    return _combine(dest, x1, rw, ys, gf)


def _pad_rows8(c):
    return jnp.pad(c, ((0, 0), (SUBLANES - c.shape[1], 0), (0, 0)))


def _lane_row(v):
    return jnp.pad(v.astype(F32), (0, LANES - v.shape[0])).reshape(1, LANES)


def _run(x, prev_a, prev_qkv, s0, p, dims):
    b, l, d = x.shape
    dc, dq, dv, nh, dk, dvh, n_exp, n_grp = dims
    xf = x.reshape(b * l, d)
    va, qkvc, z, ga, gb, ab, tail_a, tail_q = _inproj(
        xf, l, p["g1"], p["w_main"], p["w_ab"], _pad_rows8(prev_a), _pad_rows8(prev_qkv),
        p["conv_a_w"], p["conv_qkv_w"], dc=dc, dq=dq, dv=dv, d=d)
    r3 = lambda a: a.reshape(b, l, a.shape[-1])
    og, s_fin = _gdn(r3(qkvc), r3(ab), r3(z), s0, p["alog"], p["dtb"], p["onorm_g"], nh=nh, dk=dk, dv=dvh)
    x1, t, ri, rw, counts = _post(xf, va, og.reshape(b * l, dv), ga, gb, p["w_out_a"], p["w_out_b"], p["w_o"],
                                  p["g2"], p["w_r"], p["b_r"], n_exp=n_exp, n_grp=n_grp)
    y = _moe(x1, t, ri, rw, counts, p["wg"], p["wu"], p["wd"], p["gf"], n_exp=n_exp)
    last = lambda tl, w: tl.reshape(b, -1, SUBLANES, tl.shape[-1])[:, -1, SUBLANES - w:, :]
    return y.reshape(b, l, d), last(tail_a, prev_a.shape[1]), last(tail_q, prev_qkv.shape[1]), s_fin


def kernel(x_prompt, x_sample, cache_conv_a, cache_conv_qkv, state_gdn, norm1_g, w_in, conv_a_w, w_out_a, conv_qkv_w, a_log, dt_bias, onorm_g, w_out_b, w_o, norm2_g, w_router_group, b_router_group, w_router_expert, b_router_expert, w_gate, w_up, w_down, final_g):
    depth = w_in.shape[0]
    assert depth == 1, "single trunk layer"
    d = x_prompt.shape[-1]
    dc = conv_a_w.shape[-1]
    dq = conv_qkv_w.shape[-1]
    nh = a_log.shape[-1]
    dk, dvh = state_gdn.shape[-2], state_gdn.shape[-1]
    dv = nh * dvh
    n_grp = w_router_group.shape[-1]
    n_exp = w_router_expert.shape[-1]
    dims = (dc, dq, dv, nh, dk, dvh, n_exp, n_grp)

    w = w_in[0]
    o = [0, dc, 2 * dc, 3 * dc, 3 * dc + dq, 3 * dc + dq + dv, 3 * dc + dq + dv + nh, 3 * dc + dq + dv + 2 * nh,
         3 * dc + dq + dv + 2 * nh + d, 3 * dc + dq + dv + 2 * nh + 2 * d]
    seg = lambda k: w[:, o[k]:o[k + 1]]
    w_main = jnp.concatenate([seg(0), seg(2), seg(1), seg(3), seg(4), seg(7), seg(8)], axis=1).astype(BF16)
    w_ab = jnp.pad(jnp.concatenate([seg(5), seg(6)], axis=1), ((0, 0), (0, LANES - 2 * nh))).astype(BF16)
    w_r = jnp.pad(jnp.concatenate([w_router_expert[0], w_router_group[0]], axis=1),
                  ((0, 0), (0, LANES - n_exp - n_grp))).astype(BF16)
    b_r = _lane_row(jnp.concatenate([b_router_expert[0], b_router_group[0]]))
    p = dict(
        g1=norm1_g[0].reshape(1, d), w_main=w_main, w_ab=w_ab,
        conv_a_w=conv_a_w[0], w_out_a=w_out_a[0].astype(BF16),
        conv_qkv_w=conv_qkv_w[0], alog=_lane_row(a_log[0]), dtb=_lane_row(dt_bias[0]),
        onorm_g=onorm_g[0].reshape(1, dvh),
        w_out_b=w_out_b[0].astype(BF16), w_o=w_o[0].astype(BF16), g2=norm2_g[0].reshape(1, d),
        w_r=w_r, b_r=b_r,
        wg=w_gate[0], wu=w_up[0], wd=w_down[0],
        gf=final_g.reshape(1, d),
    )
    bp = x_prompt.shape[0]
    dt_ = x_prompt.dtype
    zero_a = jnp.zeros((bp,) + cache_conv_a.shape[2:], dt_)
    zero_qkv = jnp.zeros((bp,) + cache_conv_qkv.shape[2:], dt_)
    zero_s = jnp.zeros((bp,) + state_gdn.shape[2:], dt_)
    y_p, a_p, q_p, s_p = _run(x_prompt, zero_a, zero_qkv, zero_s, p, dims)
    y_s, a_s, q_s, s_s = _run(x_sample, cache_conv_a[0], cache_conv_qkv[0], state_gdn[0], p, dims)
    return (y_p, y_s, a_p[None], q_p[None], s_p[None], a_s[None], q_s[None], s_s[None])
```

```python
import functools

import jax
import jax.numpy as jnp
from jax import lax
from jax.experimental import pallas as pl
from jax.experimental.pallas import tpu as pltpu

F32 = jnp.float32
BF16 = jnp.bfloat16
EPS = 1e-6
CHUNK = 64
LANES = 128
SUBLANES = 8
VMEM_LIMIT_BYTES = 56 * 1024 * 1024
TOKEN_TILE = 512
SUB_TILE = 256
EXPERT_TILE = 512
SCATTER_TILE = 2048
GATHER_TILE = 256


def _params(n_axes):
    return pltpu.CompilerParams(dimension_semantics=("arbitrary",) * n_axes,
                                vmem_limit_bytes=VMEM_LIMIT_BYTES)


def _resident(shape):
    return pl.BlockSpec(shape, lambda *_: (0,) * len(shape), pipeline_mode=pl.Buffered(1))


def _dot(a, b):
    return jnp.dot(a, b, preferred_element_type=F32)


def _sigmoid(x):
    return 0.5 * jnp.tanh(0.5 * x) + 0.5


def _silu(x):
    h = 0.5 * x
    return h * (1.0 + jnp.tanh(h))


def _store_row_tiles(ref, x):
    n = x.shape[0]
    for c in range(SUBLANES):
        ref[pl.ds(c, n, stride=SUBLANES), :] = x[:, c * LANES:(c + 1) * LANES]


def _load_row_tiles(ref, n):
    return jnp.concatenate([ref[pl.ds(c, n, stride=SUBLANES), :] for c in range(SUBLANES)], axis=1)


def _causal_conv(u, prev8, w_ref, col0):
    width = w_ref.shape[0]
    n, c = u.shape
    tap = lambda j: w_ref[j:j + 1, col0:col0 + c]
    u1 = pltpu.roll(u, 1, 0)
    body = None
    for m in range((width + 1) // 2):
        hi = width - 1 - 2 * m
        pair = tap(hi) * u
        if hi >= 1:
            pair = pair + tap(hi - 1) * u1
        body = pair if m == 0 else body + pltpu.roll(pair, 2 * m, 0)
    head_in = jnp.concatenate([prev8, u[0:SUBLANES]], axis=0)
    lo = SUBLANES - (width - 1)
    head = tap(0) * head_in[lo:lo + SUBLANES]
    for j in range(1, width):
        head = head + tap(j) * head_in[lo + j:lo + j + SUBLANES]
    if n == SUBLANES:
        return head
    return jnp.concatenate([head, body[SUBLANES:]], axis=0)


def _inproj_body(x_ref, g_ref, w_ref, wab_ref, ca_ref, cq_ref, cwa_ref, cwq_ref,
                 va_ref, qkvc_ref, z_ref, ga_ref, gb_ref, ab_ref, ta_ref, tq_ref,
                 carry_a, carry_q, *, dc, dq, seg, nseg, tiles_per_seq, cw):
    i = pl.program_id(0)
    starts_seq = (i % tiles_per_seq) == 0
    x = x_ref[...]
    r = lax.rsqrt(jnp.mean(x * x, axis=-1, keepdims=True) + EPS)
    xn = (x * r * g_ref[...]).astype(BF16)

    def mm(lo, n):
        return _dot(xn, w_ref[:, lo:lo + n])

    def conv(val, cache_ref, carry_ref, tail_ref, cw_ref, col0):
        c = val.shape[1]
        outs = []
        for s in range(nseg):
            v = val[s * seg:(s + 1) * seg]
            prev8 = jnp.where(starts_seq, cache_ref[s, :, col0:col0 + c], carry_ref[s, :, col0:col0 + c])
            outs.append(_causal_conv(v, prev8, cw_ref, col0))
            last8 = v[seg - SUBLANES:seg]
            carry_ref[s, :, col0:col0 + c] = last8
            tail_ref[s, :, col0:col0 + c] = last8
        return outs[0] if nseg == 1 else jnp.concatenate(outs, axis=0)

    def mixer_a_task(lo):
        def products():
            return mm(lo, cw), mm(dc + lo, cw), mm(2 * dc + lo, cw)

        def finish(h, c, b):
            va_ref[:, lo:lo + cw] = (b * conv(c * h, ca_ref, carry_a, ta_ref, cwa_ref, lo)).astype(BF16)
        return products, finish

    def qkv_task(lo):
        def products():
            return (mm(3 * dc + lo, cw),)

        def finish(raw):
            qc = conv(raw, cq_ref, carry_q, tq_ref, cwq_ref, lo)
            qkvc_ref[:, lo:lo + cw] = _silu(qc).astype(BF16)
        return products, finish

    def plain_task(ref, o, lo):
        def products():
            return (mm(o + lo, cw),)

        def finish(raw):
            ref[:, lo:lo + cw] = raw.astype(BF16)
        return products, finish

    heavy = [qkv_task(lo) for lo in range(0, dq, cw)] + [mixer_a_task(lo) for lo in range(0, dc, cw)]
    light = []
    o = 3 * dc + dq
    for ref in (z_ref, ga_ref, gb_ref):
        light += [plain_task(ref, o, lo) for lo in range(0, ref.shape[1], cw)]
        o += ref.shape[1]
    tasks = []
    for k in range(max(len(heavy), len(light))):
        tasks += heavy[k:k + 1] + light[k:k + 1]
    ready = tasks[0][0]()
    for k, (_, finish) in enumerate(tasks):
        nxt = tasks[k + 1][0]() if k + 1 < len(tasks) else None
        finish(*ready)
        ready = nxt
    ab_ref[...] = _dot(xn, wab_ref[...])


def _inproj(x, l, g1, w_main, w_ab, cache_a8, cache_q8, conv_a_w, conv_qkv_w, *, dc, dq, dv, d):
    t = x.shape[0]
    tm = min(TOKEN_TILE, t)
    seg = min(tm, l)
    nseg = tm // seg
    tiles_per_seq = l // seg
    n = t // tm
    row = lambda c: pl.BlockSpec((tm, c), lambda i: (i, 0))
    cache = lambda c: pl.BlockSpec((nseg, SUBLANES, c), lambda i: (i // tiles_per_seq, 0, 0))
    tail = lambda c: pl.BlockSpec((nseg, SUBLANES, c), lambda i: (i, 0, 0))
    out_shapes = (
        jax.ShapeDtypeStruct((t, dc), BF16), jax.ShapeDtypeStruct((t, dq), BF16), jax.ShapeDtypeStruct((t, dv), BF16),
        jax.ShapeDtypeStruct((t, d), BF16), jax.ShapeDtypeStruct((t, d), BF16), jax.ShapeDtypeStruct((t, LANES), F32),
        jax.ShapeDtypeStruct((n * nseg, SUBLANES, dc), F32), jax.ShapeDtypeStruct((n * nseg, SUBLANES, dq), F32))
    return pl.pallas_call(
        functools.partial(_inproj_body, dc=dc, dq=dq, seg=seg, nseg=nseg, tiles_per_seq=tiles_per_seq, cw=SUB_TILE),
        grid=(n,),
        in_specs=[row(d), _resident((1, d)), _resident(w_main.shape), _resident(w_ab.shape),
                  cache(dc), cache(dq), _resident(conv_a_w.shape), _resident(conv_qkv_w.shape)],
        out_specs=(row(dc), row(dq), row(dv), row(d), row(d), row(LANES), tail(dc), tail(dq)),
        out_shape=out_shapes,
        scratch_shapes=[pltpu.VMEM((nseg, SUBLANES, dc), F32), pltpu.VMEM((nseg, SUBLANES, dq), F32)],
        compiler_params=_params(1),
        name="inproj",
    )(x, g1, w_main, w_ab, cache_a8, cache_q8, conv_a_w, conv_qkv_w)


def _unit_lower_inverse(mats):
    n = mats[0].shape[0]
    r = lax.broadcasted_iota(jnp.int32, (n, n), 0)
    c = lax.broadcasted_iota(jnp.int32, (n, n), 1)
    eye = (r == c).astype(F32)

    def same_block(size):
        return jnp.bitwise_xor(r, c) < size

    bb = lambda m: m.astype(BF16)
    a8 = [jnp.where(same_block(SUBLANES), a, 0.0) for a in mats]
    p2 = [bb(_dot(bb(a), bb(a))) for a in a8]
    p4 = [bb(_dot(p, p)) for p in p2]
    xs = [eye - a for a in a8]
    xs = [x + _dot(bb(x), p) for x, p in zip(xs, p2)]
    xs = [x + _dot(bb(x), p) for x, p in zip(xs, p4)]
    size = SUBLANES
    while size < n:
        in_pair = same_block(2 * size) & jnp.logical_not(same_block(size))
        ys = [_dot(bb(jnp.where(in_pair, a, 0.0)), bb(x)) for a, x in zip(mats, xs)]
        xs = [x - _dot(bb(x), bb(y)) for x, y in zip(xs, ys)]
        size *= 2
    return xs


def _gdn_body(qkv_ref, ab_ref, z_ref, s0_ref, alog_ref, dtb_ref, og_ref_g,
              og_ref, sfin_ref, s_ref, *, nb, nh, dk, dv):
    j = pl.program_id(1)
    last = pl.num_programs(1) - 1
    n = CHUNK
    bb = lambda m: m.astype(BF16)

    @pl.when(j == 0)
    def _():
        s_ref[...] = s0_ref[...]

    rows = lax.broadcasted_iota(jnp.int32, (n, LANES), 0)
    ri = lax.broadcasted_iota(jnp.int32, (n, n), 0)
    ci = lax.broadcasted_iota(jnp.int32, (n, n), 1)
    incl = ri >= ci
    strict = ri > ci

    qs, ks, vs, bcols, gcols, gls, decays = [], [], [], [], [], [], []
    for s in range(nb):
        xc = qkv_ref[s].astype(F32)
        ab = ab_ref[s]
        beta = _sigmoid(ab)
        a_in = pltpu.roll(ab, LANES - nh, 1) + dtb_ref[...]
        softplus = jnp.maximum(a_in, 0.0) + jnp.log(1.0 + jnp.exp(-jnp.abs(a_in)))
        gc = -jnp.exp(alog_ref[...]) * softplus
        sh = 1
        while sh < n:
            gc = gc + jnp.where(rows >= sh, pltpu.roll(gc, sh, 0), 0.0)
            sh *= 2
        gct = gc.T
        for h in range(nh):
            q = xc[:, h * dk:(h + 1) * dk]
            k = xc[:, nh * dk + h * dk:nh * dk + (h + 1) * dk]
            qs.append(q * (lax.rsqrt(jnp.sum(q * q, axis=-1, keepdims=True) + EPS) * (dk ** -0.5)))
            ks.append(k * lax.rsqrt(jnp.sum(k * k, axis=-1, keepdims=True) + EPS))
            vs.append(xc[:, 2 * nh * dk + h * dv:2 * nh * dk + (h + 1) * dv])
            bcols.append(beta[:, h:h + 1])
            gcol = gc[:, h:h + 1]
            gcols.append(gcol)
            gls.append(gc[n - 1:n, h:h + 1])
            decays.append(jnp.where(incl, jnp.exp(jnp.where(incl, gcol - gct[h:h + 1, :], 0.0)), 0.0))

    np_ = nb * nh
    qk_kk = [lax.dot_general(bb(jnp.concatenate([qs[p], ks[p]], axis=0)), bb(ks[p]),
                             (((1,), (1,)), ((), ())), preferred_element_type=F32) for p in range(np_)]
    attn = [bb(qk_kk[p][:n] * decays[p]) for p in range(np_)]
    ts = _unit_lower_inverse([jnp.where(strict, bcols[p] * qk_kk[p][n:] * decays[p], 0.0) for p in range(np_)])
    egs = [jnp.exp(g) for g in gcols]
    uw = [_dot(bb(ts[p]), bb(jnp.concatenate([bcols[p] * vs[p], (bcols[p] * egs[p]) * ks[p]], axis=1)))
          for p in range(np_)]
    kdt = [bb((ks[p] * jnp.exp(gls[p] - gcols[p])).T) for p in range(np_)]

    sidx = [(p // nh, p % nh) for p in range(np_)]
    s_old = [s_ref[a, h] for a, h in sidx]
    ws = [_dot(bb(jnp.concatenate([uw[p][:, dv:], qs[p] * egs[p]], axis=0)), bb(s_old[p])) for p in range(np_)]
    v_new = [bb(uw[p][:, :dv] - ws[p][:n]) for p in range(np_)]
    o_in = [_dot(attn[p], v_new[p]) for p in range(np_)]
    ds = [_dot(kdt[p], v_new[p]) for p in range(np_)]
    for p, (a, h) in enumerate(sidx):
        s_ref[a, h] = s_old[p] * jnp.exp(gls[p]) + ds[p]
        o = ws[p][n:] + o_in[p]
        on = o * lax.rsqrt(jnp.mean(o * o, axis=-1, keepdims=True) + EPS) * og_ref_g[...]
        zz = z_ref[a, :, h * dv:(h + 1) * dv].astype(F32)
        og_ref[a, :, h * dv:(h + 1) * dv] = (on * _silu(zz)).astype(BF16)

    @pl.when(j == last)
    def _():
        sfin_ref[...] = s_ref[...]


GDN_SEQS_PER_STEP = 4


def _gdn(qkvc, ab, z, s0, alog, dtb, onorm_g, *, nh, dk, dv):
    b, l, dq = qkvc.shape
    n = CHUNK
    nb = GDN_SEQS_PER_STEP if b % GDN_SEQS_PER_STEP == 0 else 1
    blk = lambda c: pl.BlockSpec((nb, n, c), lambda i, j: (i, j, 0))
    state = pl.BlockSpec((nb, nh, dk, dv), lambda i, j: (i, 0, 0, 0))
    return pl.pallas_call(
        functools.partial(_gdn_body, nb=nb, nh=nh, dk=dk, dv=dv),
        grid=(b // nb, l // n),
        in_specs=[blk(dq), blk(LANES), blk(nh * dv), state,
                  _resident(alog.shape), _resident(dtb.shape), _resident(onorm_g.shape)],
        out_specs=(blk(nh * dv), state),
        out_shape=(jax.ShapeDtypeStruct((b, l, nh * dv), BF16), jax.ShapeDtypeStruct((b, nh, dk, dv), F32)),
        scratch_shapes=[pltpu.VMEM((nb, nh, dk, dv), F32)],
        compiler_params=_params(2),
        name="gdn",
    )(qkvc, ab, z, s0, alog, dtb, onorm_g)


def _post_body(x_ref, va_ref, og_ref, ga_ref, gb_ref, woa_ref, wob_ref, wo_ref, g2_ref, wr_ref, br_ref,
               x1_ref, t_ref, ri_ref, rw_ref, cnt_ref, base_ref, *, n_exp, n_grp, epg, sub):
    i = pl.program_id(0)
    tm, d = x_ref.shape

    @pl.when(i == 0)
    def _():
        base_ref[...] = jnp.zeros_like(base_ref)

    chunks = [slice(c * sub, (c + 1) * sub) for c in range(d // sub)]

    def merged(cs):
        y_a = _dot(va_ref[...], woa_ref[:, cs])
        y_b = _dot(og_ref[...], wob_ref[:, cs])
        return (_sigmoid(ga_ref[:, cs].astype(F32)) * y_a + _sigmoid(gb_ref[:, cs].astype(F32)) * y_b).astype(BF16)

    mix = None
    pending = None
    for cs in chunks + [None]:
        m = merged(cs) if cs is not None else None
        if pending is not None:
            part = _dot(pending[1], wo_ref[pending[0], :])
            mix = part if mix is None else mix + part
        pending = (cs, m)
    x1 = x_ref[...] + mix
    x1_ref[...] = x1
    r = lax.rsqrt(jnp.mean(x1 * x1, axis=-1, keepdims=True) + EPS)
    t = x1 * r * g2_ref[...]
    _store_row_tiles(t_ref, t)
    logits = _dot(t.astype(BF16), wr_ref[...]) + br_ref[...]
    lane = lax.broadcasted_iota(jnp.int32, (tm, LANES), 1)
    lane_f = lane.astype(F32)
    neg = -jnp.inf

    def first_max(v):
        mx = jnp.max(v, axis=-1, keepdims=True)
        idx = jnp.min(jnp.where(v == mx, lane_f, float(LANES)), axis=-1, keepdims=True)
        return mx, idx.astype(jnp.int32)

    gmask = (lane >= n_exp) & (lane < n_exp + n_grp)
    gmax, gidx = first_max(jnp.where(gmask, logits, neg))
    gsel = gidx - n_exp
    pg_sel = 1.0 / jnp.sum(jnp.where(gmask, jnp.exp(logits - gmax), 0.0), axis=-1, keepdims=True)
    emask = (lane >= gsel * epg) & (lane < (gsel + 1) * epg)
    le = jnp.where(emask, logits, neg)
    m1, e1 = first_max(le)
    m2, e2 = first_max(jnp.where(lane == e1, neg, le))
    ex = jnp.exp(m2 - m1)
    p1 = 1.0 / (1.0 + ex)
    p2 = ex * p1
    onehot = ((lane == e1) | (lane == e2)).astype(BF16)
    rr = lax.broadcasted_iota(jnp.int32, (tm, tm), 0)
    cc = lax.broadcasted_iota(jnp.int32, (tm, tm), 1)
    before = _dot((rr > cc).astype(BF16), onehot) + base_ref[0:1, :]
    rank1 = jnp.sum(jnp.where(lane == e1, before, 0.0), axis=-1, keepdims=True).astype(jnp.int32)
    rank2 = jnp.sum(jnp.where(lane == e2, before, 0.0), axis=-1, keepdims=True).astype(jnp.int32)
    base_ref[0:1, :] = base_ref[0:1, :] + jnp.sum(onehot.astype(F32), axis=0, keepdims=True)
    ri = jnp.where(lane == 0, e1, jnp.where(lane == 1, e2, jnp.where(lane == 2, rank1, jnp.where(lane == 3, rank2, 0))))
    ri_ref[...] = ri.T[0:SUBLANES, :]
    rw_ref[...] = jnp.where(lane == 0, pg_sel * p1, jnp.where(lane == 1, pg_sel * p2, 0.0))
    cnt_ref[...] = base_ref[...]


def _post(x, va, og, ga, gb, w_out_a, w_out_b, w_o, g2, w_r, b_r, *, n_exp, n_grp):
    t, d = x.shape
    tm = min(TOKEN_TILE, t)
    row = lambda n: pl.BlockSpec((tm, n), lambda i: (i, 0))
    return pl.pallas_call(
        functools.partial(_post_body, n_exp=n_exp, n_grp=n_grp, epg=n_exp // n_grp, sub=SUB_TILE),
        grid=(t // tm,),
        in_specs=[row(d), row(va.shape[1]), row(og.shape[1]), row(d), row(d),
                  _resident(w_out_a.shape), _resident(w_out_b.shape), _resident(w_o.shape),
                  _resident(g2.shape), _resident(w_r.shape), _resident(b_r.shape)],
        out_specs=(row(d), pl.BlockSpec((tm * SUBLANES, LANES), lambda i: (i, 0)),
                   pl.BlockSpec((SUBLANES, tm), lambda i: (0, i)), row(LANES),
                   pl.BlockSpec((SUBLANES, LANES), lambda i: (0, 0))),
        out_shape=(jax.ShapeDtypeStruct((t, d), F32), jax.ShapeDtypeStruct((t * SUBLANES, LANES), F32),
                   jax.ShapeDtypeStruct((SUBLANES, t), jnp.int32), jax.ShapeDtypeStruct((t, LANES), F32),
                   jax.ShapeDtypeStruct((SUBLANES, LANES), F32)),
        scratch_shapes=[pltpu.VMEM((SUBLANES, LANES), F32)],
        compiler_params=_params(1),
        name="post",
    )(x, va, og, ga, gb, w_out_a, w_out_b, w_o, g2, w_r, b_r)


ROW_UNROLL = 8
TILE_ROWS = SUBLANES


def _scatter_body(ztile_ref, dest_ref, t_ref, xs_ref, zero_ref, buf_ref, sem, lsem, zsem, *, ts, r, n_zero):
    i = pl.program_id(0)
    n = pl.num_programs(0)
    rows = ts * TILE_ROWS

    def zero_copy(e):
        start = pl.multiple_of(ztile_ref[e] * TILE_ROWS, r * TILE_ROWS)
        return pltpu.make_async_copy(zero_ref, xs_ref.at[pl.ds(start, r * TILE_ROWS)], zsem)

    def load(step, slot):
        return pltpu.make_async_copy(t_ref.at[pl.ds(pl.multiple_of(step * rows, rows), rows)], buf_ref.at[slot], lsem.at[slot])

    def wait_rows(slot):
        for _ in range(2):
            pltpu.make_async_copy(buf_ref.at[slot], xs_ref.at[pl.ds(0, rows)], sem.at[slot]).wait()

    @pl.when(i == 0)
    def _():
        load(0, 0).start()
        zero_ref[...] = jnp.zeros_like(zero_ref)
        for e in range(n_zero):
            @pl.when(ztile_ref[e] >= 0)
            def _():
                zero_copy(e).start()
        for e in range(n_zero):
            @pl.when(ztile_ref[e] >= 0)
            def _():
                zero_copy(e).wait()

    slot = i % 2
    load(i, slot).wait()

    def start(q, c):
        src = buf_ref.at[slot, pl.ds(pl.multiple_of(q * TILE_ROWS, TILE_ROWS), TILE_ROWS)]
        for k in range(2):
            dst = pl.multiple_of(dest_ref[0, 0, k * ts + q] * TILE_ROWS, TILE_ROWS)
            pltpu.make_async_copy(src, xs_ref.at[pl.ds(dst, TILE_ROWS)], sem.at[slot]).start(priority=k)
        return c

    lax.fori_loop(0, ts, start, 0, unroll=ROW_UNROLL)

    @pl.when(i > 0)
    def _():
        wait_rows(1 - slot)

    @pl.when(i + 1 < n)
    def _():
        load(i + 1, 1 - slot).start()

    @pl.when(i == n - 1)
    def _():
        wait_rows(slot)


def _scatter(zero_tile_start, dest, t_tiles, n_rows, r):
    n_tok = t_tiles.shape[0] // TILE_ROWS
    ts = min(SCATTER_TILE, n_tok)
    dest3 = _per_step(dest, ts)
    grid_spec = pltpu.PrefetchScalarGridSpec(
        num_scalar_prefetch=1,
        grid=(n_tok // ts,),
        in_specs=[pl.BlockSpec((1, 1, 2 * ts), lambda i, zt: (i, 0, 0), memory_space=pltpu.SMEM),
                  pl.BlockSpec(memory_space=pl.ANY)],
        out_specs=pl.BlockSpec(memory_space=pl.ANY),
        scratch_shapes=[pltpu.VMEM((r * TILE_ROWS, LANES), F32), pltpu.VMEM((2, ts * TILE_ROWS, LANES), F32),
                        pltpu.SemaphoreType.DMA((2,)), pltpu.SemaphoreType.DMA((2,)), pltpu.SemaphoreType.DMA(())],
    )
    return pl.pallas_call(
        functools.partial(_scatter_body, ts=ts, r=r, n_zero=zero_tile_start.shape[0]),
        grid_spec=grid_spec,
        out_shape=jax.ShapeDtypeStruct((n_rows * TILE_ROWS, LANES), F32),
        compiler_params=_params(1),
        name="moe_scatter",
    )(zero_tile_start, dest3, t_tiles)


def _experts_body(te_ref, nt_ref, xs_ref, wg_ref, wu_ref, wd_ref, ys_ref, *, r, sub):
    i = pl.program_id(0)

    @pl.when(i < nt_ref[0])
    def _():
        wg = wg_ref[...].astype(BF16)
        wu = wu_ref[...].astype(BF16)
        wd = wd_ref[...].astype(BF16)
        views = [pl.ds(s * sub * TILE_ROWS, sub * TILE_ROWS) for s in range(r // sub)]
        xs = [_load_row_tiles(xs_ref.at[v], sub).astype(BF16) for v in views]
        hg = [_dot(x, wg) for x in xs]
        hu = [_dot(x, wu) for x in xs]
        hdn = [(_silu(g) * u).astype(BF16) for g, u in zip(hg, hu)]
        ys = [_dot(h, wd) for h in hdn]
        for v, y in zip(views, ys):
            _store_row_tiles(ys_ref.at[v], y)

    @pl.when(i >= nt_ref[0])
    def _():
        ys_ref[...] = jnp.zeros_like(ys_ref)


def _experts(tile_expert, n_tiles_used, xs_tiles, wg, wu, wd, r):
    rows = xs_tiles.shape[0] // TILE_ROWS
    d, de = wg.shape[1], wg.shape[2]
    used = lambda i, nt: jnp.minimum(i, nt[0] - 1)
    grid_spec = pltpu.PrefetchScalarGridSpec(
        num_scalar_prefetch=2,
        grid=(rows // r,),
        in_specs=[pl.BlockSpec((r * TILE_ROWS, LANES), lambda i, te, nt: (used(i, nt), 0)),
                  pl.BlockSpec((None, d, de), lambda i, te, nt: (te[i], 0, 0)),
                  pl.BlockSpec((None, d, de), lambda i, te, nt: (te[i], 0, 0)),
                  pl.BlockSpec((None, de, d), lambda i, te, nt: (te[i], 0, 0))],
        out_specs=pl.BlockSpec((r * TILE_ROWS, LANES), lambda i, te, nt: (i, 0)),
    )
    return pl.pallas_call(
        functools.partial(_experts_body, r=r, sub=r),
        grid_spec=grid_spec,
        out_shape=jax.ShapeDtypeStruct(xs_tiles.shape, F32),
        compiler_params=_params(1),
        name="moe_experts",
    )(tile_expert, n_tiles_used, xs_tiles, wg, wu, wd)


def _combine_body(dnext_ref, dfirst_ref, x1_ref, rw_ref, ys_ref, gf_ref, y_ref, buf_ref, sem, *, ts):
    i = pl.program_id(0)
    n = pl.num_programs(0)

    def gather(dref, slot):
        def start(q, c):
            for k in range(2):
                src = pl.multiple_of(dref[0, 0, k * ts + q] * TILE_ROWS, TILE_ROWS)
                dst = buf_ref.at[slot, k, pl.ds(pl.multiple_of(q * TILE_ROWS, TILE_ROWS), TILE_ROWS)]
                pltpu.make_async_copy(ys_ref.at[pl.ds(src, TILE_ROWS)], dst, sem.at[slot]).start(priority=k)
            return c
        lax.fori_loop(0, ts, start, 0, unroll=ROW_UNROLL)

    @pl.when(i == 0)
    def _():
        gather(dfirst_ref, 0)

    @pl.when(i + 1 < n)
    def _():
        gather(dnext_ref, (i + 1) % 2)

    slot = i % 2
    for k in range(2):
        pltpu.make_async_copy(ys_ref.at[pl.ds(0, ts * TILE_ROWS)], buf_ref.at[slot, k], sem.at[slot]).wait()
    rw = rw_ref[...]
    acc = rw[:, 0:1] * _load_row_tiles(buf_ref.at[slot, 0], ts) + rw[:, 1:2] * _load_row_tiles(buf_ref.at[slot, 1], ts)
    x2 = x1_ref[...] + acc
    r = lax.rsqrt(jnp.mean(x2 * x2, axis=-1, keepdims=True) + EPS)
    y_ref[...] = x2 * r * gf_ref[...]


def _combine(dest, x1, rw, ys_tiles, gf):
    n_tok, d = x1.shape
    ts = min(GATHER_TILE, n_tok)
    n = n_tok // ts
    dest3 = _per_step(dest, ts)
    return pl.pallas_call(
        functools.partial(_combine_body, ts=ts),
        grid=(n,),
        in_specs=[pl.BlockSpec((1, 1, 2 * ts), lambda i: (jnp.minimum(i + 1, n - 1), 0, 0), memory_space=pltpu.SMEM),
                  pl.BlockSpec((1, 1, 2 * ts), lambda i: (0, 0, 0), memory_space=pltpu.SMEM),
                  pl.BlockSpec((ts, d), lambda i: (i, 0)),
                  pl.BlockSpec((ts, LANES), lambda i: (i, 0)),
                  pl.BlockSpec(memory_space=pl.ANY),
                  _resident(gf.shape)],
        out_specs=pl.BlockSpec((ts, d), lambda i: (i, 0)),
        out_shape=jax.ShapeDtypeStruct((n_tok, d), F32),
        scratch_shapes=[pltpu.VMEM((2, 2, ts * TILE_ROWS, LANES), F32), pltpu.SemaphoreType.DMA((2,))],
        compiler_params=_params(1),
        name="moe_combine",
    )(dest3, dest3, x1, rw, ys_tiles, gf)


def _per_step(dest, ts):
    n = dest.shape[1] // ts
    return dest.reshape(2, n, ts).transpose(1, 0, 2).reshape(n, 1, 2 * ts)


def _expert_tile(n_tok, n_exp):
    r = EXPERT_TILE
    while r > LANES and r > (2 * n_tok) // n_exp:
        r //= 2
    return r


def _moe(x1, t, ri, rw, counts, wg, wu, wd, gf, *, n_exp):
    n_tok, d = x1.shape
    r = _expert_tile(n_tok, n_exp)
    cnt = counts[0, :n_exp].astype(jnp.int32)
    padded = ((cnt + r - 1) // r) * r
    ends = jnp.cumsum(padded)
    offs = ends - padded
    n_tiles = (2 * n_tok) // r + n_exp
    tile_start = jnp.arange(n_tiles, dtype=jnp.int32) * r
    tile_expert = jnp.minimum(jnp.sum(tile_start[:, None] >= ends[None, :], axis=1), n_exp - 1).astype(jnp.int32)
    n_used = (ends[-1] // r).astype(jnp.int32).reshape(1)
    tail = ends[-1] + jnp.arange(n_exp, dtype=jnp.int32) * r
    zero_tile_start = jnp.concatenate([jnp.where(cnt > 0, ends - r, -1),
                                       jnp.where(tail < n_tiles * r, tail, -1)]).astype(jnp.int32)
    experts = jnp.arange(n_exp, dtype=jnp.int32)[:, None, None]
    seg_start = jnp.sum(jnp.where(ri[None, 0:2, :] == experts, offs[:, None, None], 0), axis=0)
    dest = seg_start + ri[2:4, :]
    assert d == SUBLANES * LANES, "row-tile layout holds one 1024-wide f32 row per (8, 128) tile"
    xs = _scatter(zero_tile_start, dest, t, n_tiles * r, r)
    ys = _experts(tile_expert, n_used, xs, wg, wu, wd, r)
    return _combine(dest, x1, rw, ys, gf)


def _pad_rows8(c):
    return jnp.pad(c, ((0, 0), (SUBLANES - c.shape[1], 0), (0, 0)))


def _lane_row(v):
    return jnp.pad(v.astype(F32), (0, LANES - v.shape[0])).reshape(1, LANES)


def _run(x, prev_a, prev_qkv, s0, p, dims):
    b, l, d = x.shape
    dc, dq, dv, nh, dk, dvh, n_exp, n_grp = dims
    xf = x.reshape(b * l, d)
    va, qkvc, z, ga, gb, ab, tail_a, tail_q = _inproj(
        xf, l, p["g1"], p["w_main"], p["w_ab"], _pad_rows8(prev_a), _pad_rows8(prev_qkv),
        p["conv_a_w"], p["conv_qkv_w"], dc=dc, dq=dq, dv=dv, d=d)
    r3 = lambda a: a.reshape(b, l, a.shape[-1])
    og, s_fin = _gdn(r3(qkvc), r3(ab), r3(z), s0, p["alog"], p["dtb"], p["onorm_g"], nh=nh, dk=dk, dv=dvh)
    x1, t, ri, rw, counts = _post(xf, va, og.reshape(b * l, dv), ga, gb, p["w_out_a"], p["w_out_b"], p["w_o"],
                                  p["g2"], p["w_r"], p["b_r"], n_exp=n_exp, n_grp=n_grp)
    y = _moe(x1, t, ri, rw, counts, p["wg"], p["wu"], p["wd"], p["gf"], n_exp=n_exp)
    last = lambda tl, w: tl.reshape(b, -1, SUBLANES, tl.shape[-1])[:, -1, SUBLANES - w:, :]
    return y.reshape(b, l, d), last(tail_a, prev_a.shape[1]), last(tail_q, prev_qkv.shape[1]), s_fin


def kernel(x_prompt, x_sample, cache_conv_a, cache_conv_qkv, state_gdn, norm1_g, w_in, conv_a_w, w_out_a, conv_qkv_w, a_log, dt_bias, onorm_g, w_out_b, w_o, norm2_g, w_router_group, b_router_group, w_router_expert, b_router_expert, w_gate, w_up, w_down, final_g):
    depth = w_in.shape[0]
    assert depth == 1, "single trunk layer"
    d = x_prompt.shape[-1]
    dc = conv_a_w.shape[-1]
    dq = conv_qkv_w.shape[-1]
    nh = a_log.shape[-1]
    dk, dvh = state_gdn.shape[-2], state_gdn.shape[-1]
    dv = nh * dvh
    n_grp = w_router_group.shape[-1]
    n_exp = w_router_expert.shape[-1]
    dims = (dc, dq, dv, nh, dk, dvh, n_exp, n_grp)

    w = w_in[0]
    o = [0, dc, 2 * dc, 3 * dc, 3 * dc + dq, 3 * dc + dq + dv, 3 * dc + dq + dv + nh, 3 * dc + dq + dv + 2 * nh,
         3 * dc + dq + dv + 2 * nh + d, 3 * dc + dq + dv + 2 * nh + 2 * d]
    seg = lambda k: w[:, o[k]:o[k + 1]]
    w_main = jnp.concatenate([seg(0), seg(2), seg(1), seg(3), seg(4), seg(7), seg(8)], axis=1).astype(BF16)
    w_ab = jnp.pad(jnp.concatenate([seg(5), seg(6)], axis=1), ((0, 0), (0, LANES - 2 * nh))).astype(BF16)
    w_r = jnp.pad(jnp.concatenate([w_router_expert[0], w_router_group[0]], axis=1),
                  ((0, 0), (0, LANES - n_exp - n_grp))).astype(BF16)
    b_r = _lane_row(jnp.concatenate([b_router_expert[0], b_router_group[0]]))
    p = dict(
        g1=norm1_g[0].reshape(1, d), w_main=w_main, w_ab=w_ab,
        conv_a_w=conv_a_w[0], w_out_a=w_out_a[0].astype(BF16),
        conv_qkv_w=conv_qkv_w[0], alog=_lane_row(a_log[0]), dtb=_lane_row(dt_bias[0]),
        onorm_g=onorm_g[0].reshape(1, dvh),
        w_out_b=w_out_b[0].astype(BF16), w_o=w_o[0].astype(BF16), g2=norm2_g[0].reshape(1, d),
        w_r=w_r, b_r=b_r,
        wg=w_gate[0], wu=w_up[0], wd=w_down[0],
        gf=final_g.reshape(1, d),
    )
    bp = x_prompt.shape[0]
    dt_ = x_prompt.dtype
    zero_a = jnp.zeros((bp,) + cache_conv_a.shape[2:], dt_)
    zero_qkv = jnp.zeros((bp,) + cache_conv_qkv.shape[2:], dt_)
    zero_s = jnp.zeros((bp,) + state_gdn.shape[2:], dt_)
    y_p, a_p, q_p, s_p = _run(x_prompt, zero_a, zero_qkv, zero_s, p, dims)
    y_s, a_s, q_s, s_s = _run(x_sample, cache_conv_a[0], cache_conv_qkv[0], state_gdn[0], p, dims)
    return (y_p, y_s, a_p[None], q_p[None], s_p[None], a_s[None], q_s[None], s_s[None])
```

```python
import functools

import jax
import jax.numpy as jnp
from jax import lax
from jax.experimental import pallas as pl
from jax.experimental.pallas import tpu as pltpu

F32 = jnp.float32
BF16 = jnp.bfloat16
EPS = 1e-6
CHUNK = 64
LANES = 128
SUBLANES = 8
VMEM_LIMIT_BYTES = 56 * 1024 * 1024
TOKEN_TILE = 512
SUB_TILE = 256
EXPERT_TILE = 512
SCATTER_TILE = 2048
GATHER_TILE = 256


def _params(n_axes):
    return pltpu.CompilerParams(dimension_semantics=("arbitrary",) * n_axes,
                                vmem_limit_bytes=VMEM_LIMIT_BYTES)


def _resident(shape):
    return pl.BlockSpec(shape, lambda *_: (0,) * len(shape), pipeline_mode=pl.Buffered(1))


def _dot(a, b):
    return jnp.dot(a, b, preferred_element_type=F32)


def _sigmoid(x):
    return 0.5 * jnp.tanh(0.5 * x) + 0.5


def _silu(x):
    h = 0.5 * x
    return h * (1.0 + jnp.tanh(h))


def _store_row_tiles(ref, x):
    n = x.shape[0]
    for c in range(SUBLANES):
        ref[pl.ds(c, n, stride=SUBLANES), :] = x[:, c * LANES:(c + 1) * LANES]


def _load_row_tiles(ref, n):
    return jnp.concatenate([ref[pl.ds(c, n, stride=SUBLANES), :] for c in range(SUBLANES)], axis=1)


def _causal_conv(u, prev8, w_ref, col0):
    width = w_ref.shape[0]
    n, c = u.shape
    tap = lambda j: w_ref[j:j + 1, col0:col0 + c]
    u1 = pltpu.roll(u, 1, 0)
    body = None
    for m in range((width + 1) // 2):
        hi = width - 1 - 2 * m
        pair = tap(hi) * u
        if hi >= 1:
            pair = pair + tap(hi - 1) * u1
        body = pair if m == 0 else body + pltpu.roll(pair, 2 * m, 0)
    head_in = jnp.concatenate([prev8, u[0:SUBLANES]], axis=0)
    lo = SUBLANES - (width - 1)
    head = tap(0) * head_in[lo:lo + SUBLANES]
    for j in range(1, width):
        head = head + tap(j) * head_in[lo + j:lo + j + SUBLANES]
    if n == SUBLANES:
        return head
    return jnp.concatenate([head, body[SUBLANES:]], axis=0)


def _inproj_body(x_ref, g_ref, w_ref, wab_ref, ca_ref, cq_ref, cwa_ref, cwq_ref,
                 va_ref, qkvc_ref, z_ref, ga_ref, gb_ref, ab_ref, ta_ref, tq_ref,
                 carry_a, carry_q, *, dc, dq, seg, nseg, tiles_per_seq, cw):
    i = pl.program_id(0)
    starts_seq = (i % tiles_per_seq) == 0
    x = x_ref[...]
    r = lax.rsqrt(jnp.mean(x * x, axis=-1, keepdims=True) + EPS)
    xn = (x * r * g_ref[...]).astype(BF16)

    def mm(lo, n):
        return _dot(xn, w_ref[:, lo:lo + n])

    def conv(val, cache_ref, carry_ref, tail_ref, cw_ref, col0):
        c = val.shape[1]
        outs = []
        for s in range(nseg):
            v = val[s * seg:(s + 1) * seg]
            prev8 = jnp.where(starts_seq, cache_ref[s, :, col0:col0 + c], carry_ref[s, :, col0:col0 + c])
            outs.append(_causal_conv(v, prev8, cw_ref, col0))
            last8 = v[seg - SUBLANES:seg]
            carry_ref[s, :, col0:col0 + c] = last8
            tail_ref[s, :, col0:col0 + c] = last8
        return outs[0] if nseg == 1 else jnp.concatenate(outs, axis=0)

    def mixer_a_task(lo):
        def products():
            return mm(lo, cw), mm(dc + lo, cw), mm(2 * dc + lo, cw)

        def finish(h, c, b):
            va_ref[:, lo:lo + cw] = (b * conv(c * h, ca_ref, carry_a, ta_ref, cwa_ref, lo)).astype(BF16)
        return products, finish

    def qkv_task(lo):
        def products():
            return (mm(3 * dc + lo, cw),)

        def finish(raw):
            qc = conv(raw, cq_ref, carry_q, tq_ref, cwq_ref, lo)
            qkvc_ref[:, lo:lo + cw] = _silu(qc).astype(BF16)
        return products, finish

    def plain_task(ref, o, lo):
        def products():
            return (mm(o + lo, cw),)

        def finish(raw):
            ref[:, lo:lo + cw] = raw.astype(BF16)
        return products, finish

    heavy = [qkv_task(lo) for lo in range(0, dq, cw)] + [mixer_a_task(lo) for lo in range(0, dc, cw)]
    light = []
    o = 3 * dc + dq
    for ref in (z_ref, ga_ref, gb_ref):
        light += [plain_task(ref, o, lo) for lo in range(0, ref.shape[1], cw)]
        o += ref.shape[1]
    tasks = []
    for k in range(max(len(heavy), len(light))):
        tasks += heavy[k:k + 1] + light[k:k + 1]
    ready = tasks[0][0]()
    for k, (_, finish) in enumerate(tasks):
        nxt = tasks[k + 1][0]() if k + 1 < len(tasks) else None
        finish(*ready)
        ready = nxt
    ab_ref[...] = _dot(xn, wab_ref[...])


def _inproj(x, l, g1, w_main, w_ab, cache_a8, cache_q8, conv_a_w, conv_qkv_w, *, dc, dq, dv, d):
    t = x.shape[0]
    tm = min(TOKEN_TILE, t)
    seg = min(tm, l)
    nseg = tm // seg
    tiles_per_seq = l // seg
    n = t // tm
    row = lambda c: pl.BlockSpec((tm, c), lambda i: (i, 0))
    cache = lambda c: pl.BlockSpec((nseg, SUBLANES, c), lambda i: (i // tiles_per_seq, 0, 0))
    tail = lambda c: pl.BlockSpec((nseg, SUBLANES, c), lambda i: (i, 0, 0))
    out_shapes = (
        jax.ShapeDtypeStruct((t, dc), BF16), jax.ShapeDtypeStruct((t, dq), BF16), jax.ShapeDtypeStruct((t, dv), BF16),
        jax.ShapeDtypeStruct((t, d), BF16), jax.ShapeDtypeStruct((t, d), BF16), jax.ShapeDtypeStruct((t, LANES), F32),
        jax.ShapeDtypeStruct((n * nseg, SUBLANES, dc), F32), jax.ShapeDtypeStruct((n * nseg, SUBLANES, dq), F32))
    return pl.pallas_call(
        functools.partial(_inproj_body, dc=dc, dq=dq, seg=seg, nseg=nseg, tiles_per_seq=tiles_per_seq, cw=SUB_TILE),
        grid=(n,),
        in_specs=[row(d), _resident((1, d)), _resident(w_main.shape), _resident(w_ab.shape),
                  cache(dc), cache(dq), _resident(conv_a_w.shape), _resident(conv_qkv_w.shape)],
        out_specs=(row(dc), row(dq), row(dv), row(d), row(d), row(LANES), tail(dc), tail(dq)),
        out_shape=out_shapes,
        scratch_shapes=[pltpu.VMEM((nseg, SUBLANES, dc), F32), pltpu.VMEM((nseg, SUBLANES, dq), F32)],
        compiler_params=_params(1),
        name="inproj",
    )(x, g1, w_main, w_ab, cache_a8, cache_q8, conv_a_w, conv_qkv_w)


def _unit_lower_inverse(mats):
    n = mats[0].shape[0]
    r = lax.broadcasted_iota(jnp.int32, (n, n), 0)
    c = lax.broadcasted_iota(jnp.int32, (n, n), 1)
    eye = (r == c).astype(F32)

    def same_block(size):
        return jnp.bitwise_xor(r, c) < size

    bb = lambda m: m.astype(BF16)
    a8 = [jnp.where(same_block(SUBLANES), a, 0.0) for a in mats]
    p2 = [bb(_dot(bb(a), bb(a))) for a in a8]
    p4 = [bb(_dot(p, p)) for p in p2]
    xs = [eye - a for a in a8]
    xs = [x + _dot(bb(x), p) for x, p in zip(xs, p2)]
    xs = [x + _dot(bb(x), p) for x, p in zip(xs, p4)]
    size = SUBLANES
    while size < n:
        in_pair = same_block(2 * size) & jnp.logical_not(same_block(size))
        ys = [_dot(bb(jnp.where(in_pair, a, 0.0)), bb(x)) for a, x in zip(mats, xs)]
        xs = [x - _dot(bb(x), bb(y)) for x, y in zip(xs, ys)]
        size *= 2
    return xs


def _gdn_body(qkv_ref, ab_ref, z_ref, s0_ref, alog_ref, dtb_ref, og_ref_g,
              og_ref, sfin_ref, s_ref, *, nb, nh, dk, dv):
    j = pl.program_id(1)
    last = pl.num_programs(1) - 1
    n = CHUNK
    bb = lambda m: m.astype(BF16)

    @pl.when(j == 0)
    def _():
        s_ref[...] = s0_ref[...]

    rows = lax.broadcasted_iota(jnp.int32, (n, LANES), 0)
    ri = lax.broadcasted_iota(jnp.int32, (n, n), 0)
    ci = lax.broadcasted_iota(jnp.int32, (n, n), 1)
    incl = ri >= ci
    strict = ri > ci

    qs, ks, vs, bcols, gcols, gls, decays = [], [], [], [], [], [], []
    for s in range(nb):
        xc = qkv_ref[s].astype(F32)
        ab = ab_ref[s]
        beta = _sigmoid(ab)
        a_in = pltpu.roll(ab, LANES - nh, 1) + dtb_ref[...]
        softplus = jnp.maximum(a_in, 0.0) + jnp.log(1.0 + jnp.exp(-jnp.abs(a_in)))
        gc = -jnp.exp(alog_ref[...]) * softplus
        sh = 1
        while sh < n:
            gc = gc + jnp.where(rows >= sh, pltpu.roll(gc, sh, 0), 0.0)
            sh *= 2
        gct = gc.T
        for h in range(nh):
            q = xc[:, h * dk:(h + 1) * dk]
            k = xc[:, nh * dk + h * dk:nh * dk + (h + 1) * dk]
            qs.append(q * (lax.rsqrt(jnp.sum(q * q, axis=-1, keepdims=True) + EPS) * (dk ** -0.5)))
            ks.append(k * lax.rsqrt(jnp.sum(k * k, axis=-1, keepdims=True) + EPS))
            vs.append(xc[:, 2 * nh * dk + h * dv:2 * nh * dk + (h + 1) * dv])
            bcols.append(beta[:, h:h + 1])
            gcol = gc[:, h:h + 1]
            gcols.append(gcol)
            gls.append(gc[n - 1:n, h:h + 1])
            decays.append(jnp.where(incl, jnp.exp(jnp.where(incl, gcol - gct[h:h + 1, :], 0.0)), 0.0))

    np_ = nb * nh
    qk_kk = [lax.dot_general(bb(jnp.concatenate([qs[p], ks[p]], axis=0)), bb(ks[p]),
                             (((1,), (1,)), ((), ())), preferred_element_type=F32) for p in range(np_)]
    attn = [bb(qk_kk[p][:n] * decays[p]) for p in range(np_)]
    ts = _unit_lower_inverse([jnp.where(strict, bcols[p] * qk_kk[p][n:] * decays[p], 0.0) for p in range(np_)])
    egs = [jnp.exp(g) for g in gcols]
    uw = [_dot(bb(ts[p]), bb(jnp.concatenate([bcols[p] * vs[p], (bcols[p] * egs[p]) * ks[p]], axis=1)))
          for p in range(np_)]
    kdt = [bb((ks[p] * jnp.exp(gls[p] - gcols[p])).T) for p in range(np_)]

    sidx = [(p // nh, p % nh) for p in range(np_)]
    s_old = [s_ref[a, h] for a, h in sidx]
    ws = [_dot(bb(jnp.concatenate([uw[p][:, dv:], qs[p] * egs[p]], axis=0)), bb(s_old[p])) for p in range(np_)]
    v_new = [bb(uw[p][:, :dv] - ws[p][:n]) for p in range(np_)]
    o_in = [_dot(attn[p], v_new[p]) for p in range(np_)]
    ds = [_dot(kdt[p], v_new[p]) for p in range(np_)]
    for p, (a, h) in enumerate(sidx):
        s_ref[a, h] = s_old[p] * jnp.exp(gls[p]) + ds[p]
        o = ws[p][n:] + o_in[p]
        on = o * lax.rsqrt(jnp.mean(o * o, axis=-1, keepdims=True) + EPS) * og_ref_g[...]
        zz = z_ref[a, :, h * dv:(h + 1) * dv].astype(F32)
        og_ref[a, :, h * dv:(h + 1) * dv] = (on * _silu(zz)).astype(BF16)

    @pl.when(j == last)
    def _():
        sfin_ref[...] = s_ref[...]


GDN_SEQS_PER_STEP = 4


def _gdn(qkvc, ab, z, s0, alog, dtb, onorm_g, *, nh, dk, dv):
    b, l, dq = qkvc.shape
    n = CHUNK
    nb = GDN_SEQS_PER_STEP if b % GDN_SEQS_PER_STEP == 0 else 1
    blk = lambda c: pl.BlockSpec((nb, n, c), lambda i, j: (i, j, 0))
    state = pl.BlockSpec((nb, nh, dk, dv), lambda i, j: (i, 0, 0, 0))
    return pl.pallas_call(
        functools.partial(_gdn_body, nb=nb, nh=nh, dk=dk, dv=dv),
        grid=(b // nb, l // n),
        in_specs=[blk(dq), blk(LANES), blk(nh * dv), state,
                  _resident(alog.shape), _resident(dtb.shape), _resident(onorm_g.shape)],
        out_specs=(blk(nh * dv), state),
        out_shape=(jax.ShapeDtypeStruct((b, l, nh * dv), BF16), jax.ShapeDtypeStruct((b, nh, dk, dv), F32)),
        scratch_shapes=[pltpu.VMEM((nb, nh, dk, dv), F32)],
        compiler_params=_params(2),
        name="gdn",
    )(qkvc, ab, z, s0, alog, dtb, onorm_g)


def _post_body(x_ref, va_ref, og_ref, ga_ref, gb_ref, woa_ref, wob_ref, wo_ref, g2_ref, wr_ref, br_ref,
               x1_ref, t_ref, ri_ref, rw_ref, cnt_ref, base_ref, *, n_exp, n_grp, epg, sub):
    i = pl.program_id(0)
    tm, d = x_ref.shape

    @pl.when(i == 0)
    def _():
        base_ref[...] = jnp.zeros_like(base_ref)

    chunks = [slice(c * sub, (c + 1) * sub) for c in range(d // sub)]

    def merged(cs):
        y_a = _dot(va_ref[...], woa_ref[:, cs])
        y_b = _dot(og_ref[...], wob_ref[:, cs])
        return (_sigmoid(ga_ref[:, cs].astype(F32)) * y_a + _sigmoid(gb_ref[:, cs].astype(F32)) * y_b).astype(BF16)

    mix = None
    pending = None
    for cs in chunks + [None]:
        m = merged(cs) if cs is not None else None
        if pending is not None:
            part = _dot(pending[1], wo_ref[pending[0], :])
            mix = part if mix is None else mix + part
        pending = (cs, m)
    x1 = x_ref[...] + mix
    x1_ref[...] = x1
    r = lax.rsqrt(jnp.mean(x1 * x1, axis=-1, keepdims=True) + EPS)
    t = x1 * r * g2_ref[...]
    _store_row_tiles(t_ref, t)
    logits = _dot(t.astype(BF16), wr_ref[...]) + br_ref[...]
    lane = lax.broadcasted_iota(jnp.int32, (tm, LANES), 1)
    lane_f = lane.astype(F32)
    neg = -jnp.inf

    def first_max(v):
        mx = jnp.max(v, axis=-1, keepdims=True)
        idx = jnp.min(jnp.where(v == mx, lane_f, float(LANES)), axis=-1, keepdims=True)
        return mx, idx.astype(jnp.int32)

    gmask = (lane >= n_exp) & (lane < n_exp + n_grp)
    gmax, gidx = first_max(jnp.where(gmask, logits, neg))
    gsel = gidx - n_exp
    pg_sel = 1.0 / jnp.sum(jnp.where(gmask, jnp.exp(logits - gmax), 0.0), axis=-1, keepdims=True)
    emask = (lane >= gsel * epg) & (lane < (gsel + 1) * epg)
    le = jnp.where(emask, logits, neg)
    m1, e1 = first_max(le)
    m2, e2 = first_max(jnp.where(lane == e1, neg, le))
    ex = jnp.exp(m2 - m1)
    p1 = 1.0 / (1.0 + ex)
    p2 = ex * p1
    onehot = ((lane == e1) | (lane == e2)).astype(BF16)
    rr = lax.broadcasted_iota(jnp.int32, (tm, tm), 0)
    cc = lax.broadcasted_iota(jnp.int32, (tm, tm), 1)
    before = _dot((rr > cc).astype(BF16), onehot) + base_ref[0:1, :]
    rank1 = jnp.sum(jnp.where(lane == e1, before, 0.0), axis=-1, keepdims=True).astype(jnp.int32)
    rank2 = jnp.sum(jnp.where(lane == e2, before, 0.0), axis=-1, keepdims=True).astype(jnp.int32)
    base_ref[0:1, :] = base_ref[0:1, :] + jnp.sum(onehot.astype(F32), axis=0, keepdims=True)
    ri = jnp.where(lane == 0, e1, jnp.where(lane == 1, e2, jnp.where(lane == 2, rank1, jnp.where(lane == 3, rank2, 0))))
    ri_ref[...] = ri.T[0:SUBLANES, :]
    rw_ref[...] = jnp.where(lane == 0, pg_sel * p1, jnp.where(lane == 1, pg_sel * p2, 0.0))
    cnt_ref[...] = base_ref[...]


def _post(x, va, og, ga, gb, w_out_a, w_out_b, w_o, g2, w_r, b_r, *, n_exp, n_grp):
    t, d = x.shape
    tm = min(TOKEN_TILE, t)
    row = lambda n: pl.BlockSpec((tm, n), lambda i: (i, 0))
    return pl.pallas_call(
        functools.partial(_post_body, n_exp=n_exp, n_grp=n_grp, epg=n_exp // n_grp, sub=SUB_TILE),
        grid=(t // tm,),
        in_specs=[row(d), row(va.shape[1]), row(og.shape[1]), row(d), row(d),
                  _resident(w_out_a.shape), _resident(w_out_b.shape), _resident(w_o.shape),
                  _resident(g2.shape), _resident(w_r.shape), _resident(b_r.shape)],
        out_specs=(row(d), pl.BlockSpec((tm * SUBLANES, LANES), lambda i: (i, 0)),
                   pl.BlockSpec((SUBLANES, tm), lambda i: (0, i)), row(LANES),
                   pl.BlockSpec((SUBLANES, LANES), lambda i: (0, 0))),
        out_shape=(jax.ShapeDtypeStruct((t, d), F32), jax.ShapeDtypeStruct((t * SUBLANES, LANES), F32),
                   jax.ShapeDtypeStruct((SUBLANES, t), jnp.int32), jax.ShapeDtypeStruct((t, LANES), F32),
                   jax.ShapeDtypeStruct((SUBLANES, LANES), F32)),
        scratch_shapes=[pltpu.VMEM((SUBLANES, LANES), F32)],
        compiler_params=_params(1),
        name="post",
    )(x, va, og, ga, gb, w_out_a, w_out_b, w_o, g2, w_r, b_r)


ROW_UNROLL = 8
TILE_ROWS = SUBLANES


def _scatter_body(ztile_ref, dest_ref, t_ref, xs_ref, zero_ref, buf_ref, sem, lsem, zsem, tsem, *, ts, r, n_zero):
    i = pl.program_id(0)
    n = pl.num_programs(0)
    rows = ts * TILE_ROWS
    n_seg = n_zero // 2

    def zero_copy(e):
        start = pl.multiple_of(ztile_ref[e] * TILE_ROWS, r * TILE_ROWS)
        return pltpu.make_async_copy(zero_ref, xs_ref.at[pl.ds(start, r * TILE_ROWS)], zsem if e < n_seg else tsem)

    def wait_zero(lo, hi):
        for e in range(lo, hi):
            @pl.when(ztile_ref[e] >= 0)
            def _():
                zero_copy(e).wait()

    def load(step, slot):
        return pltpu.make_async_copy(t_ref.at[pl.ds(pl.multiple_of(step * rows, rows), rows)], buf_ref.at[slot], lsem.at[slot])

    def wait_rows(slot):
        for _ in range(2):
            pltpu.make_async_copy(buf_ref.at[slot], xs_ref.at[pl.ds(0, rows)], sem.at[slot]).wait()

    @pl.when(i == 0)
    def _():
        load(0, 0).start()
        zero_ref[...] = jnp.zeros_like(zero_ref)
        for e in range(n_zero):
            @pl.when(ztile_ref[e] >= 0)
            def _():
                zero_copy(e).start()
        wait_zero(0, n_seg)

    slot = i % 2
    load(i, slot).wait()

    def start(q, c):
        src = buf_ref.at[slot, pl.ds(pl.multiple_of(q * TILE_ROWS, TILE_ROWS), TILE_ROWS)]
        for k in range(2):
            dst = pl.multiple_of(dest_ref[0, 0, k * ts + q] * TILE_ROWS, TILE_ROWS)
            pltpu.make_async_copy(src, xs_ref.at[pl.ds(dst, TILE_ROWS)], sem.at[slot]).start(priority=k)
        return c

    lax.fori_loop(0, ts, start, 0, unroll=ROW_UNROLL)

    @pl.when(i > 0)
    def _():
        wait_rows(1 - slot)

    @pl.when(i + 1 < n)
    def _():
        load(i + 1, 1 - slot).start()

    @pl.when(i == n - 1)
    def _():
        wait_rows(slot)
        wait_zero(n_seg, n_zero)


def _scatter(zero_tile_start, dest, t_tiles, n_rows, r):
    n_tok = t_tiles.shape[0] // TILE_ROWS
    ts = min(SCATTER_TILE, n_tok)
    dest3 = _per_step(dest, ts)
    grid_spec = pltpu.PrefetchScalarGridSpec(
        num_scalar_prefetch=1,
        grid=(n_tok // ts,),
        in_specs=[pl.BlockSpec((1, 1, 2 * ts), lambda i, zt: (i, 0, 0), memory_space=pltpu.SMEM),
                  pl.BlockSpec(memory_space=pl.ANY)],
        out_specs=pl.BlockSpec(memory_space=pl.ANY),
        scratch_shapes=[pltpu.VMEM((r * TILE_ROWS, LANES), F32), pltpu.VMEM((2, ts * TILE_ROWS, LANES), F32),
                        pltpu.SemaphoreType.DMA((2,)), pltpu.SemaphoreType.DMA((2,)), pltpu.SemaphoreType.DMA(()),
                        pltpu.SemaphoreType.DMA(())],
    )
    return pl.pallas_call(
        functools.partial(_scatter_body, ts=ts, r=r, n_zero=zero_tile_start.shape[0]),
        grid_spec=grid_spec,
        out_shape=jax.ShapeDtypeStruct((n_rows * TILE_ROWS, LANES), F32),
        compiler_params=_params(1),
        name="moe_scatter",
    )(zero_tile_start, dest3, t_tiles)


def _experts_body(te_ref, nt_ref, xs_ref, wg_ref, wu_ref, wd_ref, ys_ref, *, r, sub):
    i = pl.program_id(0)

    @pl.when(i < nt_ref[0])
    def _():
        wg = wg_ref[...].astype(BF16)
        wu = wu_ref[...].astype(BF16)
        wd = wd_ref[...].astype(BF16)
        views = [pl.ds(s * sub * TILE_ROWS, sub * TILE_ROWS) for s in range(r // sub)]
        xs = [_load_row_tiles(xs_ref.at[v], sub).astype(BF16) for v in views]
        hg = [_dot(x, wg) for x in xs]
        hu = [_dot(x, wu) for x in xs]
        hdn = [(_silu(g) * u).astype(BF16) for g, u in zip(hg, hu)]
        ys = [_dot(h, wd) for h in hdn]
        for v, y in zip(views, ys):
            _store_row_tiles(ys_ref.at[v], y)

    @pl.when(i >= nt_ref[0])
    def _():
        ys_ref[...] = jnp.zeros_like(ys_ref)


def _experts(tile_expert, n_tiles_used, xs_tiles, wg, wu, wd, r):
    rows = xs_tiles.shape[0] // TILE_ROWS
    d, de = wg.shape[1], wg.shape[2]
    used = lambda i, nt: jnp.minimum(i, nt[0] - 1)
    grid_spec = pltpu.PrefetchScalarGridSpec(
        num_scalar_prefetch=2,
        grid=(rows // r,),
        in_specs=[pl.BlockSpec((r * TILE_ROWS, LANES), lambda i, te, nt: (used(i, nt), 0)),
                  pl.BlockSpec((None, d, de), lambda i, te, nt: (te[i], 0, 0)),
                  pl.BlockSpec((None, d, de), lambda i, te, nt: (te[i], 0, 0)),
                  pl.BlockSpec((None, de, d), lambda i, te, nt: (te[i], 0, 0))],
        out_specs=pl.BlockSpec((r * TILE_ROWS, LANES), lambda i, te, nt: (i, 0)),
    )
    return pl.pallas_call(
        functools.partial(_experts_body, r=r, sub=r),
        grid_spec=grid_spec,
        out_shape=jax.ShapeDtypeStruct(xs_tiles.shape, F32),
        compiler_params=_params(1),
        name="moe_experts",
    )(tile_expert, n_tiles_used, xs_tiles, wg, wu, wd)


def _combine_body(dnext_ref, dfirst_ref, x1_ref, rw_ref, ys_ref, gf_ref, y_ref, buf_ref, sem, *, ts):
    i = pl.program_id(0)
    n = pl.num_programs(0)

    def gather(dref, slot):
        def start(q, c):
            for k in range(2):
                src = pl.multiple_of(dref[0, 0, k * ts + q] * TILE_ROWS, TILE_ROWS)
                dst = buf_ref.at[slot, k, pl.ds(pl.multiple_of(q * TILE_ROWS, TILE_ROWS), TILE_ROWS)]
                pltpu.make_async_copy(ys_ref.at[pl.ds(src, TILE_ROWS)], dst, sem.at[slot]).start(priority=k)
            return c
        lax.fori_loop(0, ts, start, 0, unroll=ROW_UNROLL)

    @pl.when(i == 0)
    def _():
        gather(dfirst_ref, 0)

    @pl.when(i + 1 < n)
    def _():
        gather(dnext_ref, (i + 1) % 2)

    slot = i % 2
    for k in range(2):
        pltpu.make_async_copy(ys_ref.at[pl.ds(0, ts * TILE_ROWS)], buf_ref.at[slot, k], sem.at[slot]).wait()
    rw = rw_ref[...]
    acc = rw[:, 0:1] * _load_row_tiles(buf_ref.at[slot, 0], ts) + rw[:, 1:2] * _load_row_tiles(buf_ref.at[slot, 1], ts)
    x2 = x1_ref[...] + acc
    r = lax.rsqrt(jnp.mean(x2 * x2, axis=-1, keepdims=True) + EPS)
    y_ref[...] = x2 * r * gf_ref[...]


def _combine(dest, x1, rw, ys_tiles, gf):
    n_tok, d = x1.shape
    ts = min(GATHER_TILE, n_tok)
    n = n_tok // ts
    dest3 = _per_step(dest, ts)
    return pl.pallas_call(
        functools.partial(_combine_body, ts=ts),
        grid=(n,),
        in_specs=[pl.BlockSpec((1, 1, 2 * ts), lambda i: (jnp.minimum(i + 1, n - 1), 0, 0), memory_space=pltpu.SMEM),
                  pl.BlockSpec((1, 1, 2 * ts), lambda i: (0, 0, 0), memory_space=pltpu.SMEM),
                  pl.BlockSpec((ts, d), lambda i: (i, 0)),
                  pl.BlockSpec((ts, LANES), lambda i: (i, 0)),
                  pl.BlockSpec(memory_space=pl.ANY),
                  _resident(gf.shape)],
        out_specs=pl.BlockSpec((ts, d), lambda i: (i, 0)),
        out_shape=jax.ShapeDtypeStruct((n_tok, d), F32),
        scratch_shapes=[pltpu.VMEM((2, 2, ts * TILE_ROWS, LANES), F32), pltpu.SemaphoreType.DMA((2,))],
        compiler_params=_params(1),
        name="moe_combine",
    )(dest3, dest3, x1, rw, ys_tiles, gf)


def _per_step(dest, ts):
    n = dest.shape[1] // ts
    return dest.reshape(2, n, ts).transpose(1, 0, 2).reshape(n, 1, 2 * ts)


def _expert_tile(n_tok, n_exp):
    r = EXPERT_TILE
    while r > LANES and r > (2 * n_tok) // n_exp:
        r //= 2
    return r


def _moe(x1, t, ri, rw, counts, wg, wu, wd, gf, *, n_exp):
    n_tok, d = x1.shape
    r = _expert_tile(n_tok, n_exp)
    cnt = counts[0, :n_exp].astype(jnp.int32)
    padded = ((cnt + r - 1) // r) * r
    ends = jnp.cumsum(padded)
    offs = ends - padded
    n_tiles = (2 * n_tok) // r + n_exp
    tile_start = jnp.arange(n_tiles, dtype=jnp.int32) * r
    tile_expert = jnp.minimum(jnp.sum(tile_start[:, None] >= ends[None, :], axis=1), n_exp - 1).astype(jnp.int32)
    n_used = (ends[-1] // r).astype(jnp.int32).reshape(1)
    tail = ends[-1] + jnp.arange(n_exp, dtype=jnp.int32) * r
    zero_tile_start = jnp.concatenate([jnp.where(cnt > 0, ends - r, -1),
                                       jnp.where(tail < n_tiles * r, tail, -1)]).astype(jnp.int32)
    experts = jnp.arange(n_exp, dtype=jnp.int32)[:, None, None]
    seg_start = jnp.sum(jnp.where(ri[None, 0:2, :] == experts, offs[:, None, None], 0), axis=0)
    dest = seg_start + ri[2:4, :]
    assert d == SUBLANES * LANES, "row-tile layout holds one 1024-wide f32 row per (8, 128) tile"
    xs = _scatter(zero_tile_start, dest, t, n_tiles * r, r)
    ys = _experts(tile_expert, n_used, xs, wg, wu, wd, r)
    return _combine(dest, x1, rw, ys, gf)


def _pad_rows8(c):
    return jnp.pad(c, ((0, 0), (SUBLANES - c.shape[1], 0), (0, 0)))


def _lane_row(v):
    return jnp.pad(v.astype(F32), (0, LANES - v.shape[0])).reshape(1, LANES)


def _run(x, prev_a, prev_qkv, s0, p, dims):
    b, l, d = x.shape
    dc, dq, dv, nh, dk, dvh, n_exp, n_grp = dims
    xf = x.reshape(b * l, d)
    va, qkvc, z, ga, gb, ab, tail_a, tail_q = _inproj(
        xf, l, p["g1"], p["w_main"], p["w_ab"], _pad_rows8(prev_a), _pad_rows8(prev_qkv),
        p["conv_a_w"], p["conv_qkv_w"], dc=dc, dq=dq, dv=dv, d=d)
    r3 = lambda a: a.reshape(b, l, a.shape[-1])
    og, s_fin = _gdn(r3(qkvc), r3(ab), r3(z), s0, p["alog"], p["dtb"], p["onorm_g"], nh=nh, dk=dk, dv=dvh)
    x1, t, ri, rw, counts = _post(xf, va, og.reshape(b * l, dv), ga, gb, p["w_out_a"], p["w_out_b"], p["w_o"],
                                  p["g2"], p["w_r"], p["b_r"], n_exp=n_exp, n_grp=n_grp)
    y = _moe(x1, t, ri, rw, counts, p["wg"], p["wu"], p["wd"], p["gf"], n_exp=n_exp)
    last = lambda tl, w: tl.reshape(b, -1, SUBLANES, tl.shape[-1])[:, -1, SUBLANES - w:, :]
    return y.reshape(b, l, d), last(tail_a, prev_a.shape[1]), last(tail_q, prev_qkv.shape[1]), s_fin


def kernel(x_prompt, x_sample, cache_conv_a, cache_conv_qkv, state_gdn, norm1_g, w_in, conv_a_w, w_out_a, conv_qkv_w, a_log, dt_bias, onorm_g, w_out_b, w_o, norm2_g, w_router_group, b_router_group, w_router_expert, b_router_expert, w_gate, w_up, w_down, final_g):
    depth = w_in.shape[0]
    assert depth == 1, "single trunk layer"
    d = x_prompt.shape[-1]
    dc = conv_a_w.shape[-1]
    dq = conv_qkv_w.shape[-1]
    nh = a_log.shape[-1]
    dk, dvh = state_gdn.shape[-2], state_gdn.shape[-1]
    dv = nh * dvh
    n_grp = w_router_group.shape[-1]
    n_exp = w_router_expert.shape[-1]
    dims = (dc, dq, dv, nh, dk, dvh, n_exp, n_grp)

    w = w_in[0]
    o = [0, dc, 2 * dc, 3 * dc, 3 * dc + dq, 3 * dc + dq + dv, 3 * dc + dq + dv + nh, 3 * dc + dq + dv + 2 * nh,
         3 * dc + dq + dv + 2 * nh + d, 3 * dc + dq + dv + 2 * nh + 2 * d]
    seg = lambda k: w[:, o[k]:o[k + 1]]
    w_main = jnp.concatenate([seg(0), seg(2), seg(1), seg(3), seg(4), seg(7), seg(8)], axis=1).astype(BF16)
    w_ab = jnp.pad(jnp.concatenate([seg(5), seg(6)], axis=1), ((0, 0), (0, LANES - 2 * nh))).astype(BF16)
    w_r = jnp.pad(jnp.concatenate([w_router_expert[0], w_router_group[0]], axis=1),
                  ((0, 0), (0, LANES - n_exp - n_grp))).astype(BF16)
    b_r = _lane_row(jnp.concatenate([b_router_expert[0], b_router_group[0]]))
    p = dict(
        g1=norm1_g[0].reshape(1, d), w_main=w_main, w_ab=w_ab,
        conv_a_w=conv_a_w[0], w_out_a=w_out_a[0].astype(BF16),
        conv_qkv_w=conv_qkv_w[0], alog=_lane_row(a_log[0]), dtb=_lane_row(dt_bias[0]),
        onorm_g=onorm_g[0].reshape(1, dvh),
        w_out_b=w_out_b[0].astype(BF16), w_o=w_o[0].astype(BF16), g2=norm2_g[0].reshape(1, d),
        w_r=w_r, b_r=b_r,
        wg=w_gate[0], wu=w_up[0], wd=w_down[0],
        gf=final_g.reshape(1, d),
    )
    bp = x_prompt.shape[0]
    dt_ = x_prompt.dtype
    zero_a = jnp.zeros((bp,) + cache_conv_a.shape[2:], dt_)
    zero_qkv = jnp.zeros((bp,) + cache_conv_qkv.shape[2:], dt_)
    zero_s = jnp.zeros((bp,) + state_gdn.shape[2:], dt_)
    y_p, a_p, q_p, s_p = _run(x_prompt, zero_a, zero_qkv, zero_s, p, dims)
    y_s, a_s, q_s, s_s = _run(x_sample, cache_conv_a[0], cache_conv_qkv[0], state_gdn[0], p, dims)
    return (y_p, y_s, a_p[None], q_p[None], s_p[None], a_s[None], q_s[None], s_s[None])
```

```python
import functools

import jax
import jax.numpy as jnp
from jax import lax
from jax.experimental import pallas as pl
from jax.experimental.pallas import tpu as pltpu

F32 = jnp.float32
BF16 = jnp.bfloat16
EPS = 1e-6
CHUNK = 64
LANES = 128
SUBLANES = 8
VMEM_LIMIT_BYTES = 56 * 1024 * 1024
TOKEN_TILE = 512
SUB_TILE = 256
EXPERT_TILE = 512
SCATTER_TILE = 2048
GATHER_TILE = 256


def _params(n_axes):
    return pltpu.CompilerParams(dimension_semantics=("arbitrary",) * n_axes,
                                vmem_limit_bytes=VMEM_LIMIT_BYTES)


def _resident(shape):
    return pl.BlockSpec(shape, lambda *_: (0,) * len(shape), pipeline_mode=pl.Buffered(1))


def _dot(a, b):
    return jnp.dot(a, b, preferred_element_type=F32)


def _sigmoid(x):
    return 0.5 * jnp.tanh(0.5 * x) + 0.5


def _silu(x):
    h = 0.5 * x
    return h * (1.0 + jnp.tanh(h))


def _store_row_tiles(ref, x):
    n = x.shape[0]
    for c in range(SUBLANES):
        ref[pl.ds(c, n, stride=SUBLANES), :] = x[:, c * LANES:(c + 1) * LANES]


def _load_row_tiles(ref, n):
    return jnp.concatenate([ref[pl.ds(c, n, stride=SUBLANES), :] for c in range(SUBLANES)], axis=1)


def _causal_conv(u, prev8, w_ref, col0):
    width = w_ref.shape[0]
    n, c = u.shape
    tap = lambda j: w_ref[j:j + 1, col0:col0 + c]
    u1 = pltpu.roll(u, 1, 0)
    body = None
    for m in range((width + 1) // 2):
        hi = width - 1 - 2 * m
        pair = tap(hi) * u
        if hi >= 1:
            pair = pair + tap(hi - 1) * u1
        body = pair if m == 0 else body + pltpu.roll(pair, 2 * m, 0)
    head_in = jnp.concatenate([prev8, u[0:SUBLANES]], axis=0)
    lo = SUBLANES - (width - 1)
    head = tap(0) * head_in[lo:lo + SUBLANES]
    for j in range(1, width):
        head = head + tap(j) * head_in[lo + j:lo + j + SUBLANES]
    if n == SUBLANES:
        return head
    return jnp.concatenate([head, body[SUBLANES:]], axis=0)


def _inproj_body(x_ref, g_ref, w_ref, wab_ref, ca_ref, cq_ref, cwa_ref, cwq_ref,
                 va_ref, qkvc_ref, z_ref, ga_ref, gb_ref, ab_ref, ta_ref, tq_ref,
                 carry_a, carry_q, *, dc, dq, seg, nseg, tiles_per_seq, cw):
    i = pl.program_id(0)
    starts_seq = (i % tiles_per_seq) == 0
    x = x_ref[...]
    r = lax.rsqrt(jnp.mean(x * x, axis=-1, keepdims=True) + EPS)
    xn = (x * r * g_ref[...]).astype(BF16)

    def mm(lo, n):
        return _dot(xn, w_ref[:, lo:lo + n])

    def conv(val, cache_ref, carry_ref, tail_ref, cw_ref, col0):
        c = val.shape[1]
        outs = []
        for s in range(nseg):
            v = val[s * seg:(s + 1) * seg]
            prev8 = jnp.where(starts_seq, cache_ref[s, :, col0:col0 + c], carry_ref[s, :, col0:col0 + c])
            outs.append(_causal_conv(v, prev8, cw_ref, col0))
            last8 = v[seg - SUBLANES:seg]
            carry_ref[s, :, col0:col0 + c] = last8
            tail_ref[s, :, col0:col0 + c] = last8
        return outs[0] if nseg == 1 else jnp.concatenate(outs, axis=0)

    def mixer_a_task(lo):
        def products():
            return mm(lo, cw), mm(dc + lo, cw), mm(2 * dc + lo, cw)

        def finish(h, c, b):
            va_ref[:, lo:lo + cw] = (b * conv(c * h, ca_ref, carry_a, ta_ref, cwa_ref, lo)).astype(BF16)
        return products, finish

    def qkv_task(lo):
        def products():
            return (mm(3 * dc + lo, cw),)

        def finish(raw):
            qc = conv(raw, cq_ref, carry_q, tq_ref, cwq_ref, lo)
            qkvc_ref[:, lo:lo + cw] = _silu(qc).astype(BF16)
        return products, finish

    def plain_task(ref, o, lo):
        def products():
            return (mm(o + lo, cw),)

        def finish(raw):
            ref[:, lo:lo + cw] = raw.astype(BF16)
        return products, finish

    heavy = [qkv_task(lo) for lo in range(0, dq, cw)] + [mixer_a_task(lo) for lo in range(0, dc, cw)]
    light = []
    o = 3 * dc + dq
    for ref in (z_ref, ga_ref, gb_ref):
        light += [plain_task(ref, o, lo) for lo in range(0, ref.shape[1], cw)]
        o += ref.shape[1]
    tasks = []
    for k in range(max(len(heavy), len(light))):
        tasks += heavy[k:k + 1] + light[k:k + 1]
    ready = tasks[0][0]()
    for k, (_, finish) in enumerate(tasks):
        nxt = tasks[k + 1][0]() if k + 1 < len(tasks) else None
        finish(*ready)
        ready = nxt
    ab_ref[...] = _dot(xn, wab_ref[...])


def _inproj(x, l, g1, w_main, w_ab, cache_a8, cache_q8, conv_a_w, conv_qkv_w, *, dc, dq, dv, d):
    t = x.shape[0]
    tm = min(TOKEN_TILE, t)
    seg = min(tm, l)
    nseg = tm // seg
    tiles_per_seq = l // seg
    n = t // tm
    row = lambda c: pl.BlockSpec((tm, c), lambda i: (i, 0))
    cache = lambda c: pl.BlockSpec((nseg, SUBLANES, c), lambda i: (i // tiles_per_seq, 0, 0))
    tail = lambda c: pl.BlockSpec((nseg, SUBLANES, c), lambda i: (i, 0, 0))
    out_shapes = (
        jax.ShapeDtypeStruct((t, dc), BF16), jax.ShapeDtypeStruct((t, dq), BF16), jax.ShapeDtypeStruct((t, dv), BF16),
        jax.ShapeDtypeStruct((t, d), BF16), jax.ShapeDtypeStruct((t, d), BF16), jax.ShapeDtypeStruct((t, LANES), F32),
        jax.ShapeDtypeStruct((n * nseg, SUBLANES, dc), F32), jax.ShapeDtypeStruct((n * nseg, SUBLANES, dq), F32))
    return pl.pallas_call(
        functools.partial(_inproj_body, dc=dc, dq=dq, seg=seg, nseg=nseg, tiles_per_seq=tiles_per_seq, cw=SUB_TILE),
        grid=(n,),
        in_specs=[row(d), _resident((1, d)), _resident(w_main.shape), _resident(w_ab.shape),
                  cache(dc), cache(dq), _resident(conv_a_w.shape), _resident(conv_qkv_w.shape)],
        out_specs=(row(dc), row(dq), row(dv), row(d), row(d), row(LANES), tail(dc), tail(dq)),
        out_shape=out_shapes,
        scratch_shapes=[pltpu.VMEM((nseg, SUBLANES, dc), F32), pltpu.VMEM((nseg, SUBLANES, dq), F32)],
        compiler_params=_params(1),
        name="inproj",
    )(x, g1, w_main, w_ab, cache_a8, cache_q8, conv_a_w, conv_qkv_w)


def _unit_lower_inverse(mats):
    n = mats[0].shape[0]
    r = lax.broadcasted_iota(jnp.int32, (n, n), 0)
    c = lax.broadcasted_iota(jnp.int32, (n, n), 1)
    eye = (r == c).astype(F32)

    def same_block(size):
        return jnp.bitwise_xor(r, c) < size

    bb = lambda m: m.astype(BF16)
    a8 = [jnp.where(same_block(SUBLANES), a, 0.0) for a in mats]
    p2 = [bb(_dot(bb(a), bb(a))) for a in a8]
    p4 = [bb(_dot(p, p)) for p in p2]
    xs = [eye - a for a in a8]
    xs = [x + _dot(bb(x), p) for x, p in zip(xs, p2)]
    xs = [x + _dot(bb(x), p) for x, p in zip(xs, p4)]
    size = SUBLANES
    while size < n:
        in_pair = same_block(2 * size) & jnp.logical_not(same_block(size))
        ys = [_dot(bb(jnp.where(in_pair, a, 0.0)), bb(x)) for a, x in zip(mats, xs)]
        xs = [x - _dot(bb(x), bb(y)) for x, y in zip(xs, ys)]
        size *= 2
    return xs


def _gdn_body(qkv_ref, ab_ref, z_ref, s0_ref, alog_ref, dtb_ref, og_ref_g,
              og_ref, sfin_ref, s_ref, *, nb, nh, dk, dv):
    j = pl.program_id(1)
    last = pl.num_programs(1) - 1
    n = CHUNK
    bb = lambda m: m.astype(BF16)

    @pl.when(j == 0)
    def _():
        s_ref[...] = s0_ref[...]

    rows = lax.broadcasted_iota(jnp.int32, (n, LANES), 0)
    ri = lax.broadcasted_iota(jnp.int32, (n, n), 0)
    ci = lax.broadcasted_iota(jnp.int32, (n, n), 1)
    incl = ri >= ci
    strict = ri > ci

    qs, ks, vs, bcols, gcols, gls, decays = [], [], [], [], [], [], []
    for s in range(nb):
        xc = qkv_ref[s].astype(F32)
        ab = ab_ref[s]
        beta = _sigmoid(ab)
        a_in = pltpu.roll(ab, LANES - nh, 1) + dtb_ref[...]
        softplus = jnp.maximum(a_in, 0.0) + jnp.log(1.0 + jnp.exp(-jnp.abs(a_in)))
        gc = -jnp.exp(alog_ref[...]) * softplus
        sh = 1
        while sh < n:
            gc = gc + jnp.where(rows >= sh, pltpu.roll(gc, sh, 0), 0.0)
            sh *= 2
        gct = gc.T
        for h in range(nh):
            q = xc[:, h * dk:(h + 1) * dk]
            k = xc[:, nh * dk + h * dk:nh * dk + (h + 1) * dk]
            qs.append(q * (lax.rsqrt(jnp.sum(q * q, axis=-1, keepdims=True) + EPS) * (dk ** -0.5)))
            ks.append(k * lax.rsqrt(jnp.sum(k * k, axis=-1, keepdims=True) + EPS))
            vs.append(xc[:, 2 * nh * dk + h * dv:2 * nh * dk + (h + 1) * dv])
            bcols.append(beta[:, h:h + 1])
            gcol = gc[:, h:h + 1]
            gcols.append(gcol)
            gls.append(gc[n - 1:n, h:h + 1])
            decays.append(jnp.where(incl, jnp.exp(jnp.where(incl, gcol - gct[h:h + 1, :], 0.0)), 0.0))

    np_ = nb * nh
    qk_kk = [lax.dot_general(bb(jnp.concatenate([qs[p], ks[p]], axis=0)), bb(ks[p]),
                             (((1,), (1,)), ((), ())), preferred_element_type=F32) for p in range(np_)]
    attn = [bb(qk_kk[p][:n] * decays[p]) for p in range(np_)]
    ts = _unit_lower_inverse([jnp.where(strict, bcols[p] * qk_kk[p][n:] * decays[p], 0.0) for p in range(np_)])
    egs = [jnp.exp(g) for g in gcols]
    uw = [_dot(bb(ts[p]), bb(jnp.concatenate([bcols[p] * vs[p], (bcols[p] * egs[p]) * ks[p]], axis=1)))
          for p in range(np_)]
    kdt = [bb((ks[p] * jnp.exp(gls[p] - gcols[p])).T) for p in range(np_)]

    sidx = [(p // nh, p % nh) for p in range(np_)]
    s_old = [s_ref[a, h] for a, h in sidx]
    ws = [_dot(bb(jnp.concatenate([uw[p][:, dv:], qs[p] * egs[p]], axis=0)), bb(s_old[p])) for p in range(np_)]
    v_new = [bb(uw[p][:, :dv] - ws[p][:n]) for p in range(np_)]
    o_in = [_dot(attn[p], v_new[p]) for p in range(np_)]
    ds = [_dot(kdt[p], v_new[p]) for p in range(np_)]
    for p, (a, h) in enumerate(sidx):
        s_ref[a, h] = s_old[p] * jnp.exp(gls[p]) + ds[p]
        o = ws[p][n:] + o_in[p]
        on = o * lax.rsqrt(jnp.mean(o * o, axis=-1, keepdims=True) + EPS) * og_ref_g[...]
        zz = z_ref[a, :, h * dv:(h + 1) * dv].astype(F32)
        og_ref[a, :, h * dv:(h + 1) * dv] = (on * _silu(zz)).astype(BF16)

    @pl.when(j == last)
    def _():
        sfin_ref[...] = s_ref[...]


GDN_SEQS_PER_STEP = 4


def _gdn(qkvc, ab, z, s0, alog, dtb, onorm_g, *, nh, dk, dv):
    b, l, dq = qkvc.shape
    n = CHUNK
    nb = GDN_SEQS_PER_STEP if b % GDN_SEQS_PER_STEP == 0 else 1
    blk = lambda c: pl.BlockSpec((nb, n, c), lambda i, j: (i, j, 0))
    state = pl.BlockSpec((nb, nh, dk, dv), lambda i, j: (i, 0, 0, 0))
    return pl.pallas_call(
        functools.partial(_gdn_body, nb=nb, nh=nh, dk=dk, dv=dv),
        grid=(b // nb, l // n),
        in_specs=[blk(dq), blk(LANES), blk(nh * dv), state,
                  _resident(alog.shape), _resident(dtb.shape), _resident(onorm_g.shape)],
        out_specs=(blk(nh * dv), state),
        out_shape=(jax.ShapeDtypeStruct((b, l, nh * dv), BF16), jax.ShapeDtypeStruct((b, nh, dk, dv), F32)),
        scratch_shapes=[pltpu.VMEM((nb, nh, dk, dv), F32)],
        compiler_params=_params(2),
        name="gdn",
    )(qkvc, ab, z, s0, alog, dtb, onorm_g)


def _post_body(x_ref, va_ref, og_ref, ga_ref, gb_ref, woa_ref, wob_ref, wo_ref, g2_ref, wr_ref, br_ref,
               x1_ref, t_ref, ri_ref, rw_ref, cnt_ref, base_ref, *, n_exp, n_grp, epg, sub):
    i = pl.program_id(0)
    tm, d = x_ref.shape

    @pl.when(i == 0)
    def _():
        base_ref[...] = jnp.zeros_like(base_ref)

    chunks = [slice(c * sub, (c + 1) * sub) for c in range(d // sub)]

    def merged(cs):
        y_a = _dot(va_ref[...], woa_ref[:, cs])
        y_b = _dot(og_ref[...], wob_ref[:, cs])
        return (_sigmoid(ga_ref[:, cs].astype(F32)) * y_a + _sigmoid(gb_ref[:, cs].astype(F32)) * y_b).astype(BF16)

    mix = None
    pending = None
    for cs in chunks + [None]:
        m = merged(cs) if cs is not None else None
        if pending is not None:
            part = _dot(pending[1], wo_ref[pending[0], :])
            mix = part if mix is None else mix + part
        pending = (cs, m)
    x1 = x_ref[...] + mix
    x1_ref[...] = x1
    r = lax.rsqrt(jnp.mean(x1 * x1, axis=-1, keepdims=True) + EPS)
    t = x1 * r * g2_ref[...]
    _store_row_tiles(t_ref, t)
    logits = _dot(t.astype(BF16), wr_ref[...]) + br_ref[...]
    lane = lax.broadcasted_iota(jnp.int32, (tm, LANES), 1)
    lane_f = lane.astype(F32)
    neg = -jnp.inf

    def first_max(v):
        mx = jnp.max(v, axis=-1, keepdims=True)
        idx = jnp.min(jnp.where(v == mx, lane_f, float(LANES)), axis=-1, keepdims=True)
        return mx, idx.astype(jnp.int32)

    gmask = (lane >= n_exp) & (lane < n_exp + n_grp)
    gmax, gidx = first_max(jnp.where(gmask, logits, neg))
    gsel = gidx - n_exp
    pg_sel = 1.0 / jnp.sum(jnp.where(gmask, jnp.exp(logits - gmax), 0.0), axis=-1, keepdims=True)
    emask = (lane >= gsel * epg) & (lane < (gsel + 1) * epg)
    le = jnp.where(emask, logits, neg)
    m1, e1 = first_max(le)
    m2, e2 = first_max(jnp.where(lane == e1, neg, le))
    ex = jnp.exp(m2 - m1)
    p1 = 1.0 / (1.0 + ex)
    p2 = ex * p1
    onehot = ((lane == e1) | (lane == e2)).astype(BF16)
    rr = lax.broadcasted_iota(jnp.int32, (tm, tm), 0)
    cc = lax.broadcasted_iota(jnp.int32, (tm, tm), 1)
    before = _dot((rr > cc).astype(BF16), onehot) + base_ref[0:1, :]
    rank1 = jnp.sum(jnp.where(lane == e1, before, 0.0), axis=-1, keepdims=True).astype(jnp.int32)
    rank2 = jnp.sum(jnp.where(lane == e2, before, 0.0), axis=-1, keepdims=True).astype(jnp.int32)
    base_ref[0:1, :] = base_ref[0:1, :] + jnp.sum(onehot.astype(F32), axis=0, keepdims=True)
    ri = jnp.where(lane == 0, e1, jnp.where(lane == 1, e2, jnp.where(lane == 2, rank1, jnp.where(lane == 3, rank2, 0))))
    ri_ref[...] = ri.T[0:SUBLANES, :]
    rw_ref[...] = jnp.where(lane == 0, pg_sel * p1, jnp.where(lane == 1, pg_sel * p2, 0.0))
    cnt_ref[...] = base_ref[...]


def _post(x, va, og, ga, gb, w_out_a, w_out_b, w_o, g2, w_r, b_r, *, n_exp, n_grp):
    t, d = x.shape
    tm = min(TOKEN_TILE, t)
    row = lambda n: pl.BlockSpec((tm, n), lambda i: (i, 0))
    return pl.pallas_call(
        functools.partial(_post_body, n_exp=n_exp, n_grp=n_grp, epg=n_exp // n_grp, sub=SUB_TILE),
        grid=(t // tm,),
        in_specs=[row(d), row(va.shape[1]), row(og.shape[1]), row(d), row(d),
                  _resident(w_out_a.shape), _resident(w_out_b.shape), _resident(w_o.shape),
                  _resident(g2.shape), _resident(w_r.shape), _resident(b_r.shape)],
        out_specs=(row(d), pl.BlockSpec((tm * SUBLANES, LANES), lambda i: (i, 0)),
                   pl.BlockSpec((SUBLANES, tm), lambda i: (0, i)), row(LANES),
                   pl.BlockSpec((SUBLANES, LANES), lambda i: (0, 0))),
        out_shape=(jax.ShapeDtypeStruct((t, d), F32), jax.ShapeDtypeStruct((t * SUBLANES, LANES), F32),
                   jax.ShapeDtypeStruct((SUBLANES, t), jnp.int32), jax.ShapeDtypeStruct((t, LANES), F32),
                   jax.ShapeDtypeStruct((SUBLANES, LANES), F32)),
        scratch_shapes=[pltpu.VMEM((SUBLANES, LANES), F32)],
        compiler_params=_params(1),
        name="post",
    )(x, va, og, ga, gb, w_out_a, w_out_b, w_o, g2, w_r, b_r)


ROW_UNROLL = 8
TILE_ROWS = SUBLANES


def _scatter_body(ztile_ref, dest_ref, t_ref, xs_ref, zero_ref, buf_ref, sem, lsem, zsem, tsem, *, ts, r, n_zero):
    i = pl.program_id(0)
    n = pl.num_programs(0)
    rows = ts * TILE_ROWS
    n_seg = n_zero // 2

    def zero_copy(e):
        start = pl.multiple_of(ztile_ref[e] * TILE_ROWS, r * TILE_ROWS)
        return pltpu.make_async_copy(zero_ref, xs_ref.at[pl.ds(start, r * TILE_ROWS)], zsem if e < n_seg else tsem)

    def wait_zero(lo, hi):
        for e in range(lo, hi):
            @pl.when(ztile_ref[e] >= 0)
            def _():
                zero_copy(e).wait()

    def load(step, slot):
        return pltpu.make_async_copy(t_ref.at[pl.ds(pl.multiple_of(step * rows, rows), rows)], buf_ref.at[slot], lsem.at[slot])

    def wait_rows(slot):
        for _ in range(2):
            pltpu.make_async_copy(buf_ref.at[slot], xs_ref.at[pl.ds(0, rows)], sem.at[slot]).wait()

    @pl.when(i == 0)
    def _():
        load(0, 0).start()
        zero_ref[...] = jnp.zeros_like(zero_ref)
        for e in range(n_zero):
            @pl.when(ztile_ref[e] >= 0)
            def _():
                zero_copy(e).start()
        wait_zero(0, n_seg)

    slot = i % 2
    load(i, slot).wait()

    def start(q, c):
        src = buf_ref.at[slot, pl.ds(pl.multiple_of(q * TILE_ROWS, TILE_ROWS), TILE_ROWS)]
        for k in range(2):
            dst = pl.multiple_of(dest_ref[0, 0, k * ts + q] * TILE_ROWS, TILE_ROWS)
            pltpu.make_async_copy(src, xs_ref.at[pl.ds(dst, TILE_ROWS)], sem.at[slot]).start(priority=k)
        return c

    lax.fori_loop(0, ts, start, 0, unroll=ROW_UNROLL)

    @pl.when(i > 0)
    def _():
        wait_rows(1 - slot)

    @pl.when(i + 1 < n)
    def _():
        load(i + 1, 1 - slot).start()

    @pl.when(i == n - 1)
    def _():
        wait_rows(slot)
        wait_zero(n_seg, n_zero)


def _scatter(zero_tile_start, dest, t_tiles, n_rows, r):
    n_tok = t_tiles.shape[0] // TILE_ROWS
    ts = min(SCATTER_TILE, n_tok)
    dest3 = _per_step(dest, ts)
    grid_spec = pltpu.PrefetchScalarGridSpec(
        num_scalar_prefetch=1,
        grid=(n_tok // ts,),
        in_specs=[pl.BlockSpec((1, 1, 2 * ts), lambda i, zt: (i, 0, 0), memory_space=pltpu.SMEM),
                  pl.BlockSpec(memory_space=pl.ANY)],
        out_specs=pl.BlockSpec(memory_space=pl.ANY),
        scratch_shapes=[pltpu.VMEM((r * TILE_ROWS, LANES), F32), pltpu.VMEM((2, ts * TILE_ROWS, LANES), F32),
                        pltpu.SemaphoreType.DMA((2,)), pltpu.SemaphoreType.DMA((2,)), pltpu.SemaphoreType.DMA(()),
                        pltpu.SemaphoreType.DMA(())],
    )
    return pl.pallas_call(
        functools.partial(_scatter_body, ts=ts, r=r, n_zero=zero_tile_start.shape[0]),
        grid_spec=grid_spec,
        out_shape=jax.ShapeDtypeStruct((n_rows * TILE_ROWS, LANES), F32),
        compiler_params=_params(1),
        name="moe_scatter",
    )(zero_tile_start, dest3, t_tiles)


def _experts_body(te_ref, nt_ref, xs_ref, wg_ref, wu_ref, wd_ref, ys_ref, *, r, sub):
    i = pl.program_id(0)

    @pl.when(i < nt_ref[0])
    def _():
        wg = wg_ref[...].astype(BF16)
        wu = wu_ref[...].astype(BF16)
        wd = wd_ref[...].astype(BF16)
        views = [pl.ds(s * sub * TILE_ROWS, sub * TILE_ROWS) for s in range(r // sub)]
        xs = [_load_row_tiles(xs_ref.at[v], sub).astype(BF16) for v in views]
        hg = [_dot(x, wg) for x in xs]
        hu = [_dot(x, wu) for x in xs]
        hdn = [(_silu(g) * u).astype(BF16) for g, u in zip(hg, hu)]
        ys = [_dot(h, wd) for h in hdn]
        for v, y in zip(views, ys):
            _store_row_tiles(ys_ref.at[v], y)

    @pl.when(i >= nt_ref[0])
    def _():
        ys_ref[...] = jnp.zeros_like(ys_ref)


def _experts(tile_expert, n_tiles_used, xs_tiles, wg, wu, wd, r):
    rows = xs_tiles.shape[0] // TILE_ROWS
    d, de = wg.shape[1], wg.shape[2]
    used = lambda i, nt: jnp.minimum(i, nt[0] - 1)
    grid_spec = pltpu.PrefetchScalarGridSpec(
        num_scalar_prefetch=2,
        grid=(rows // r,),
        in_specs=[pl.BlockSpec((r * TILE_ROWS, LANES), lambda i, te, nt: (used(i, nt), 0)),
                  pl.BlockSpec((None, d, de), lambda i, te, nt: (te[i], 0, 0)),
                  pl.BlockSpec((None, d, de), lambda i, te, nt: (te[i], 0, 0)),
                  pl.BlockSpec((None, de, d), lambda i, te, nt: (te[i], 0, 0))],
        out_specs=pl.BlockSpec((r * TILE_ROWS, LANES), lambda i, te, nt: (i, 0)),
    )
    return pl.pallas_call(
        functools.partial(_experts_body, r=r, sub=r),
        grid_spec=grid_spec,
        out_shape=jax.ShapeDtypeStruct(xs_tiles.shape, F32),
        compiler_params=_params(1),
        name="moe_experts",
    )(tile_expert, n_tiles_used, xs_tiles, wg, wu, wd)


def _combine_body(dnext_ref, dfirst_ref, x1_ref, rw_ref, ys_ref, gf_ref, y_ref, buf_ref, sem, *, ts):
    i = pl.program_id(0)
    n = pl.num_programs(0)

    def gather(dref, slot):
        def start(q, c):
            for k in range(2):
                src = pl.multiple_of(dref[0, 0, k * ts + q] * TILE_ROWS, TILE_ROWS)
                dst = buf_ref.at[slot, k, pl.ds(pl.multiple_of(q * TILE_ROWS, TILE_ROWS), TILE_ROWS)]
                pltpu.make_async_copy(ys_ref.at[pl.ds(src, TILE_ROWS)], dst, sem.at[slot]).start(priority=k)
            return c
        lax.fori_loop(0, ts, start, 0, unroll=ROW_UNROLL)

    @pl.when(i == 0)
    def _():
        gather(dfirst_ref, 0)

    @pl.when(i + 1 < n)
    def _():
        gather(dnext_ref, (i + 1) % 2)

    def finish(slot):
        for k in range(2):
            pltpu.make_async_copy(ys_ref.at[pl.ds(0, ts * TILE_ROWS)], buf_ref.at[slot, k], sem.at[slot]).wait()
        rw = rw_ref[...]
        acc = rw[:, 0:1] * _load_row_tiles(buf_ref.at[slot, 0], ts) + rw[:, 1:2] * _load_row_tiles(buf_ref.at[slot, 1], ts)
        x2 = x1_ref[...] + acc
        r = lax.rsqrt(jnp.mean(x2 * x2, axis=-1, keepdims=True) + EPS)
        y_ref[...] = x2 * r * gf_ref[...]

    for parity in (0, 1):
        @pl.when(i % 2 == parity)
        def _():
            finish(parity)


def _combine(dest, x1, rw, ys_tiles, gf):
    n_tok, d = x1.shape
    ts = min(GATHER_TILE, n_tok)
    n = n_tok // ts
    dest3 = _per_step(dest, ts)
    return pl.pallas_call(
        functools.partial(_combine_body, ts=ts),
        grid=(n,),
        in_specs=[pl.BlockSpec((1, 1, 2 * ts), lambda i: (jnp.minimum(i + 1, n - 1), 0, 0), memory_space=pltpu.SMEM),
                  pl.BlockSpec((1, 1, 2 * ts), lambda i: (0, 0, 0), memory_space=pltpu.SMEM),
                  pl.BlockSpec((ts, d), lambda i: (i, 0)),
                  pl.BlockSpec((ts, LANES), lambda i: (i, 0)),
                  pl.BlockSpec(memory_space=pl.ANY),
                  _resident(gf.shape)],
        out_specs=pl.BlockSpec((ts, d), lambda i: (i, 0)),
        out_shape=jax.ShapeDtypeStruct((n_tok, d), F32),
        scratch_shapes=[pltpu.VMEM((2, 2, ts * TILE_ROWS, LANES), F32), pltpu.SemaphoreType.DMA((2,))],
        compiler_params=_params(1),
        name="moe_combine",
    )(dest3, dest3, x1, rw, ys_tiles, gf)


def _per_step(dest, ts):
    n = dest.shape[1] // ts
    return dest.reshape(2, n, ts).transpose(1, 0, 2).reshape(n, 1, 2 * ts)


def _expert_tile(n_tok, n_exp):
    r = EXPERT_TILE
    while r > LANES and r > (2 * n_tok) // n_exp:
        r //= 2
    return r


def _moe(x1, t, ri, rw, counts, wg, wu, wd, gf, *, n_exp):
    n_tok, d = x1.shape
    r = _expert_tile(n_tok, n_exp)
    cnt = counts[0, :n_exp].astype(jnp.int32)
    padded = ((cnt + r - 1) // r) * r
    ends = jnp.cumsum(padded)
    offs = ends - padded
    n_tiles = (2 * n_tok) // r + n_exp
    tile_start = jnp.arange(n_tiles, dtype=jnp.int32) * r
    tile_expert = jnp.minimum(jnp.sum(tile_start[:, None] >= ends[None, :], axis=1), n_exp - 1).astype(jnp.int32)
    n_used = (ends[-1] // r).astype(jnp.int32).reshape(1)
    tail = ends[-1] + jnp.arange(n_exp, dtype=jnp.int32) * r
    zero_tile_start = jnp.concatenate([jnp.where(cnt > 0, ends - r, -1),
                                       jnp.where(tail < n_tiles * r, tail, -1)]).astype(jnp.int32)
    experts = jnp.arange(n_exp, dtype=jnp.int32)[:, None, None]
    seg_start = jnp.sum(jnp.where(ri[None, 0:2, :] == experts, offs[:, None, None], 0), axis=0)
    dest = seg_start + ri[2:4, :]
    assert d == SUBLANES * LANES, "row-tile layout holds one 1024-wide f32 row per (8, 128) tile"
    xs = _scatter(zero_tile_start, dest, t, n_tiles * r, r)
    ys = _experts(tile_expert, n_used, xs, wg, wu, wd, r)
    return _combine(dest, x1, rw, ys, gf)


def _pad_rows8(c):
    return jnp.pad(c, ((0, 0), (SUBLANES - c.shape[1], 0), (0, 0)))


def _lane_row(v):
    return jnp.pad(v.astype(F32), (0, LANES - v.shape[0])).reshape(1, LANES)


def _run(x, prev_a, prev_qkv, s0, p, dims):
    b, l, d = x.shape
    dc, dq, dv, nh, dk, dvh, n_exp, n_grp = dims
    xf = x.reshape(b * l, d)
    va, qkvc, z, ga, gb, ab, tail_a, tail_q = _inproj(
        xf, l, p["g1"], p["w_main"], p["w_ab"], _pad_rows8(prev_a), _pad_rows8(prev_qkv),
        p["conv_a_w"], p["conv_qkv_w"], dc=dc, dq=dq, dv=dv, d=d)
    r3 = lambda a: a.reshape(b, l, a.shape[-1])
    og, s_fin = _gdn(r3(qkvc), r3(ab), r3(z), s0, p["alog"], p["dtb"], p["onorm_g"], nh=nh, dk=dk, dv=dvh)
    x1, t, ri, rw, counts = _post(xf, va, og.reshape(b * l, dv), ga, gb, p["w_out_a"], p["w_out_b"], p["w_o"],
                                  p["g2"], p["w_r"], p["b_r"], n_exp=n_exp, n_grp=n_grp)
    y = _moe(x1, t, ri, rw, counts, p["wg"], p["wu"], p["wd"], p["gf"], n_exp=n_exp)
    last = lambda tl, w: tl.reshape(b, -1, SUBLANES, tl.shape[-1])[:, -1, SUBLANES - w:, :]
    return y.reshape(b, l, d), last(tail_a, prev_a.shape[1]), last(tail_q, prev_qkv.shape[1]), s_fin


def kernel(x_prompt, x_sample, cache_conv_a, cache_conv_qkv, state_gdn, norm1_g, w_in, conv_a_w, w_out_a, conv_qkv_w, a_log, dt_bias, onorm_g, w_out_b, w_o, norm2_g, w_router_group, b_router_group, w_router_expert, b_router_expert, w_gate, w_up, w_down, final_g):
    depth = w_in.shape[0]
    assert depth == 1, "single trunk layer"
    d = x_prompt.shape[-1]
    dc = conv_a_w.shape[-1]
    dq = conv_qkv_w.shape[-1]
    nh = a_log.shape[-1]
    dk, dvh = state_gdn.shape[-2], state_gdn.shape[-1]
    dv = nh * dvh
    n_grp = w_router_group.shape[-1]
    n_exp = w_router_expert.shape[-1]
    dims = (dc, dq, dv, nh, dk, dvh, n_exp, n_grp)

    w = w_in[0]
    o = [0, dc, 2 * dc, 3 * dc, 3 * dc + dq, 3 * dc + dq + dv, 3 * dc + dq + dv + nh, 3 * dc + dq + dv + 2 * nh,
         3 * dc + dq + dv + 2 * nh + d, 3 * dc + dq + dv + 2 * nh + 2 * d]
    seg = lambda k: w[:, o[k]:o[k + 1]]
    w_main = jnp.concatenate([seg(0), seg(2), seg(1), seg(3), seg(4), seg(7), seg(8)], axis=1).astype(BF16)
    w_ab = jnp.pad(jnp.concatenate([seg(5), seg(6)], axis=1), ((0, 0), (0, LANES - 2 * nh))).astype(BF16)
    w_r = jnp.pad(jnp.concatenate([w_router_expert[0], w_router_group[0]], axis=1),
                  ((0, 0), (0, LANES - n_exp - n_grp))).astype(BF16)
    b_r = _lane_row(jnp.concatenate([b_router_expert[0], b_router_group[0]]))
    p = dict(
        g1=norm1_g[0].reshape(1, d), w_main=w_main, w_ab=w_ab,
        conv_a_w=conv_a_w[0], w_out_a=w_out_a[0].astype(BF16),
        conv_qkv_w=conv_qkv_w[0], alog=_lane_row(a_log[0]), dtb=_lane_row(dt_bias[0]),
        onorm_g=onorm_g[0].reshape(1, dvh),
        w_out_b=w_out_b[0].astype(BF16), w_o=w_o[0].astype(BF16), g2=norm2_g[0].reshape(1, d),
        w_r=w_r, b_r=b_r,
        wg=w_gate[0], wu=w_up[0], wd=w_down[0],
        gf=final_g.reshape(1, d),
    )
    bp = x_prompt.shape[0]
    dt_ = x_prompt.dtype
    zero_a = jnp.zeros((bp,) + cache_conv_a.shape[2:], dt_)
    zero_qkv = jnp.zeros((bp,) + cache_conv_qkv.shape[2:], dt_)
    zero_s = jnp.zeros((bp,) + state_gdn.shape[2:], dt_)
    y_p, a_p, q_p, s_p = _run(x_prompt, zero_a, zero_qkv, zero_s, p, dims)
    y_s, a_s, q_s, s_s = _run(x_sample, cache_conv_a[0], cache_conv_qkv[0], state_gdn[0], p, dims)
    return (y_p, y_s, a_p[None], q_p[None], s_p[None], a_s[None], q_s[None], s_s[None])
```
